```python
import math
import jax
import jax.numpy as jnp
from jax import lax
import numpy as np

D_MODEL = 1024
BATCH = 4
SEQ = 4096
DEPTH = 4

HEAD_DIM = 64
R_HEADS = 4
R_WIDTH = R_HEADS * HEAD_DIM
R_LORA_W = 32
R_LORA_A = 32
R_LORA_G = 64
R_COLS = 3 * R_WIDTH + R_LORA_W + R_LORA_A + R_LORA_G
R_GN_EPS = 64e-5
DF_HEADS = 4
DF_QK = HEAD_DIM
DF_V = 2 * HEAD_DIM
DF_WIDTH = DF_HEADS * DF_V
DF_QCOLS = DF_HEADS * 2 * DF_QK
DF_COLS = 2 * DF_QCOLS + DF_WIDTH
MB_HEADS = 4
MB_WIDTH = MB_HEADS * HEAD_DIM
MB_COLS = 3 * MB_WIDTH
MB_BLOCK = 256
MB_TOPK = 3
MB_Q_CHUNK = 64
ATT_Q_CHUNK = 128
IN_COLS = R_COLS + DF_COLS + MB_COLS
MIX_WIDTH = R_WIDTH + DF_WIDTH + MB_WIDTH
NUM_BUCKETS = 32
MAX_DISTANCE = 1024
N_BIAS_HEADS = DF_HEADS + MB_HEADS
N_GROUPS = 4
EXPERTS_PER_GROUP = 8
N_EXPERTS = N_GROUPS * EXPERTS_PER_GROUP
TOP_K_INNER = 2
EXPERT_HIDDEN = 512
MOE_BLOCK = 256
RMS_EPS = 1e-6

kernel_name = 'hybrid_rwkv7_diffattn_moba_hmoe'

F32 = jnp.float32


def rmsnorm(x, g, eps=RMS_EPS):
    xf = x.astype(F32)
    y = xf * lax.rsqrt(jnp.mean(xf * xf, -1, keepdims=True) + eps)
    return (y * g.astype(F32)).astype(x.dtype)


def t5_bucket(dist):
    n = jnp.maximum(dist, 0)
    max_exact = NUM_BUCKETS // 2
    nf = jnp.maximum(n, 1).astype(F32)
    large = max_exact + (jnp.log(nf / max_exact) / math.log(MAX_DISTANCE / max_exact)
                         * (NUM_BUCKETS - max_exact)).astype(jnp.int32)
    large = jnp.minimum(large, NUM_BUCKETS - 1)
    return jnp.where(n < max_exact, n, large)


def rwkv7_mix(p, mu, w0, w2, a0, a2, g2, k_k, k_a, r_k, ln_g, ln_b):
    B, S, _ = p.shape
    p_prev = jnp.pad(p, ((0, 0), (1, 0), (0, 0)))[:, :-1]
    p = p + (p_prev - p) * mu
    c1 = R_WIDTH
    c2 = 2 * R_WIDTH
    c3 = 3 * R_WIDTH
    c4 = c3 + R_LORA_W
    c5 = c4 + R_LORA_A
    r, k, v, wd, ad, gd = jnp.split(p, [c1, c2, c3, c4, c5], axis=-1)
    w = -jax.nn.softplus(-(w0 + jnp.tanh(wd) @ w2)) - 0.5
    decay = jnp.exp(-jnp.exp(w.astype(F32)))
    a = jax.nn.sigmoid(a0 + ad @ a2)
    g = jax.nn.sigmoid(gd) @ g2
    heads = lambda t: t.astype(F32).reshape(B, S, R_HEADS, HEAD_DIM)
    kk = heads(k * k_k)
    kk = kk / jnp.maximum(jnp.sqrt(jnp.sum(kk * kk, -1, keepdims=True)), 1e-12)
    k = k * (1 + (a - 1) * k_a)
    r_h, k_h, v_h, a_h, w_h = heads(r), heads(k), heads(v), heads(a), heads(decay)

    def step(state, inp):
        r_t, w_t, k_t, v_t, kk_t, a_t = inp
        sa = jnp.einsum('bhvk,bhk->bhv', state, -kk_t)
        state = (state * w_t[:, :, None, :]
                 + jnp.einsum('bhv,bhk->bhvk', sa, kk_t * a_t)
                 + jnp.einsum('bhv,bhk->bhvk', v_t, k_t))
        return state, jnp.einsum('bhvk,bhk->bhv', state, r_t)

    xs = tuple(jnp.moveaxis(t, 1, 0) for t in (r_h, w_h, k_h, v_h, kk, a_h))
    s0 = jnp.zeros((B, R_HEADS, HEAD_DIM, HEAD_DIM), F32)
    _, y = lax.scan(step, s0, xs)
    y = jnp.moveaxis(y, 0, 1)
    mean = jnp.mean(y, -1, keepdims=True)
    var = jnp.mean(jnp.square(y - mean), -1, keepdims=True)
    y = ((y - mean) * lax.rsqrt(var + R_GN_EPS)).reshape(B, S, R_WIDTH)
    y = y * ln_g.astype(F32) + ln_b.astype(F32)
    bonus = jnp.sum(r_h * k_h * r_k.astype(F32), -1, keepdims=True) * v_h
    y = y + bonus.reshape(B, S, R_WIDTH)
    return (y * g.astype(F32)).astype(p.dtype)


def diff_attention(p, q_gain, k_gain, lam, subln_g, bias_tbl, lambda_init):
    B, S, _ = p.shape
    q, k, v = jnp.split(p, [DF_QCOLS, 2 * DF_QCOLS], axis=-1)
    q = rmsnorm(q.reshape(B, S, DF_HEADS, 2, DF_QK), q_gain)
    k = rmsnorm(k.reshape(B, S, DF_HEADS, 2, DF_QK), k_gain)
    v = v.reshape(B, S, DF_HEADS, DF_V).astype(F32)
    lamf = lam.astype(F32)
    lam_full = (jnp.exp(jnp.sum(lamf[0] * lamf[1])) - jnp.exp(jnp.sum(lamf[2] * lamf[3]))
                + lambda_init)
    scale = DF_QK ** -0.5
    nqb = S // ATT_Q_CHUNK
    qb = q.reshape(B, nqb, ATT_Q_CHUNK, DF_HEADS, 2, DF_QK).transpose(1, 0, 3, 4, 2, 5)
    kpos = jnp.arange(S)
    tbl = bias_tbl.astype(F32)

    def block(args):
        qc, i = args
        qpos = i * ATT_Q_CHUNK + jnp.arange(ATT_Q_CHUNK)
        dist = qpos[:, None] - kpos[None, :]
        bias = tbl[t5_bucket(dist)][..., :DF_HEADS].transpose(2, 0, 1)
        logits = jnp.einsum('bhcqd,bkhcd->bhcqk', qc, k).astype(F32) * scale + bias[None, :, None]
        logits = jnp.where(dist >= 0, logits, -jnp.inf)
        probs = jax.nn.softmax(logits, axis=-1)
        attn = probs[:, :, 0] - lam_full * probs[:, :, 1]
        return jnp.einsum('bhqk,bkhe->bqhe', attn, v)

    out = lax.map(block, (qb, jnp.arange(nqb)))
    out = out.transpose(1, 0, 2, 3, 4).reshape(B, S, DF_HEADS, DF_V)
    out = rmsnorm(out, subln_g) * (1.0 - lambda_init)
    return out.reshape(B, S, DF_WIDTH).astype(p.dtype)


def moba_attention(p, q_gain, k_gain, bias_tbl):
    B, S, _ = p.shape
    q, k, v = jnp.split(p, 3, axis=-1)
    q = rmsnorm(q.reshape(B, S, MB_HEADS, HEAD_DIM), q_gain).transpose(0, 2, 1, 3)
    k = rmsnorm(k.reshape(B, S, MB_HEADS, HEAD_DIM), k_gain).transpose(0, 2, 1, 3)
    v = v.reshape(B, S, MB_HEADS, HEAD_DIM).transpose(0, 2, 1, 3)
    nb = -(-S // MB_BLOCK)
    pad = nb * MB_BLOCK - S
    k = jnp.pad(k, ((0, 0), (0, 0), (0, pad), (0, 0)))
    v = jnp.pad(v, ((0, 0), (0, 0), (0, pad), (0, 0)))
    kb = k.reshape(B, MB_HEADS, nb, MB_BLOCK, HEAD_DIM)
    vb = v.reshape(B, MB_HEADS, nb, MB_BLOCK, HEAD_DIM)
    kmean = jnp.mean(kb.astype(F32), axis=3)
    topk = min(MB_TOPK, nb)
    scale = HEAD_DIM ** -0.5
    tbl = bias_tbl[:, DF_HEADS:].T.astype(F32)
    nqc = S // MB_Q_CHUNK
    qc_all = q.reshape(B, MB_HEADS, nqc, MB_Q_CHUNK, HEAD_DIM).transpose(2, 0, 1, 3, 4)
    bi = jnp.arange(B)[:, None, None, None]
    hi = jnp.arange(MB_HEADS)[None, :, None, None]
    blk_ids = jnp.arange(nb)
    offs = jnp.arange(MB_BLOCK)

    def chunk(args):
        qc, i = args
        qpos = i * MB_Q_CHUNK + jnp.arange(MB_Q_CHUNK)
        own = (i * MB_Q_CHUNK) // MB_BLOCK
        gate = jnp.einsum('bhqd,bhnd->bhqn', qc.astype(F32), kmean)
        gate = jnp.where(blk_ids < own, gate, -jnp.inf)
        _, idx = lax.top_k(gate, topk)
        valid = jnp.arange(topk) < own
        kg = kb[bi, hi, idx]
        vg = vb[bi, hi, idx]
        dist_sel = qpos[:, None, None] - (idx[..., None] * MB_BLOCK + offs)
        logit_sel = (jnp.einsum('bhqd,bhqnkd->bhqnk', qc, kg).astype(F32) * scale
                     + tbl[hi[..., None], t5_bucket(dist_sel)])
        logit_sel = jnp.where(valid[:, None], logit_sel, -jnp.inf)
        ko = lax.dynamic_index_in_dim(kb, own, axis=2, keepdims=False)
        vo = lax.dynamic_index_in_dim(vb, own, axis=2, keepdims=False)
        dist_own = qpos[:, None] - (own * MB_BLOCK + offs)[None, :]
        logit_own = (jnp.einsum('bhqd,bhkd->bhqk', qc, ko).astype(F32) * scale
                     + tbl[:, t5_bucket(dist_own)][None])
        logit_own = jnp.where(dist_own >= 0, logit_own, -jnp.inf)
        nq = qc.shape[2]
        logits = jnp.concatenate([logit_sel.reshape(B, MB_HEADS, nq, topk * MB_BLOCK), logit_own], -1)
        probs = jax.nn.softmax(logits, axis=-1)
        p_sel = probs[..., :topk * MB_BLOCK].reshape(B, MB_HEADS, nq, topk, MB_BLOCK)
        p_own = probs[..., topk * MB_BLOCK:]
        return (jnp.einsum('bhqnk,bhqnkd->bhqd', p_sel, vg.astype(F32))
                + jnp.einsum('bhqk,bhkd->bhqd', p_own, vo.astype(F32)))

    out = lax.map(chunk, (qc_all, jnp.arange(nqc)))
    out = out.transpose(1, 0, 3, 2, 4).reshape(B, S, MB_WIDTH)
    return out.astype(p.dtype)


def hier_moe(h, wg, bg, we, be, w1, w3, w2):
    B, S, Dm = h.shape
    T = B * S
    hf = h.reshape(T, Dm)
    g_logits = (hf @ wg + bg).astype(F32)
    pg = jax.nn.softmax(g_logits, axis=-1)
    g_idx = jnp.argmax(g_logits, axis=-1)
    pg_top = jnp.take_along_axis(pg, g_idx[:, None], axis=-1)
    e_logits = (hf @ we + be).astype(F32).reshape(T, N_GROUPS, EXPERTS_PER_GROUP)
    e_logits = jnp.take_along_axis(e_logits, g_idx[:, None, None], axis=1)[:, 0]
    pe = jax.nn.softmax(e_logits, axis=-1)
    pe_top, e_local = lax.top_k(pe, TOP_K_INNER)
    gates = pg_top * pe_top / jnp.sum(pe_top, -1, keepdims=True)
    experts = g_idx[:, None] * EXPERTS_PER_GROUP + e_local
    A = T * TOP_K_INNER
    flat_e = experts.reshape(A)
    order = jnp.argsort(flat_e)
    se = flat_e[order]
    tok = order // TOP_K_INNER
    counts = jnp.zeros((N_EXPERTS,), jnp.int32).at[flat_e].add(1)
    padded = (counts + MOE_BLOCK - 1) // MOE_BLOCK * MOE_BLOCK
    pad_end = jnp.cumsum(padded)
    pad_start = pad_end - padded
    start = jnp.cumsum(counts) - counts
    dest = pad_start[se] + jnp.arange(A) - start[se]
    n_blocks = -(-(A + N_EXPERTS * (MOE_BLOCK - 1)) // MOE_BLOCK)
    buf = jnp.zeros((n_blocks * MOE_BLOCK, Dm), h.dtype).at[dest].set(hf[tok])
    blk_e = jnp.minimum(jnp.searchsorted(pad_end, jnp.arange(n_blocks) * MOE_BLOCK, side='right'),
                        N_EXPERTS - 1)

    def expert_block(args):
        xb, e = args
        return (jax.nn.silu(xb @ w1[e]) * (xb @ w3[e])) @ w2[e]

    yb = lax.map(expert_block, (buf.reshape(n_blocks, MOE_BLOCK, Dm), blk_e))
    y_assign = yb.reshape(-1, Dm)[dest] * gates.reshape(A)[order][:, None].astype(h.dtype)
    y = jnp.zeros((T, Dm), h.dtype).at[tok].add(y_assign)
    return y.reshape(B, S, Dm)


def setup_inputs(seed: int = 0) -> dict:
    key = jax.random.key(seed)
    ks = iter(jax.random.split(key, 40))
    nrm = lambda shape, s: jax.random.normal(next(ks), shape, F32) * s
    uni = lambda shape, lo, hi: jax.random.uniform(next(ks), shape, F32, lo, hi)
    L = DEPTH
    return {
        'x': nrm((BATCH, SEQ, D_MODEL), 1.0),
        'c': nrm((BATCH, D_MODEL), 1.0),
        'ada_w': nrm((L, D_MODEL, 6 * D_MODEL), 0.5 * D_MODEL ** -0.5),
        'ada_b': nrm((L, 6 * D_MODEL), 0.01),
        'norm1_g': 1.0 + nrm((L, D_MODEL), 0.02),
        'norm2_g': 1.0 + nrm((L, D_MODEL), 0.02),
        'w_in': nrm((L, D_MODEL, IN_COLS), D_MODEL ** -0.5),
        'w_out': nrm((L, MIX_WIDTH, D_MODEL), MIX_WIDTH ** -0.5),
        'rwkv_mu': uni((L, R_COLS), 0.0, 1.0),
        'rwkv_w0': uni((L, R_WIDTH), -6.5, -1.5),
        'rwkv_w2': nrm((L, R_LORA_W, R_WIDTH), 0.1 * R_LORA_W ** -0.5),
        'rwkv_a0': nrm((L, R_WIDTH), 0.1),
        'rwkv_a2': nrm((L, R_LORA_A, R_WIDTH), 0.1 * R_LORA_A ** -0.5),
        'rwkv_g2': nrm((L, R_LORA_G, R_WIDTH), R_LORA_G ** -0.5),
        'rwkv_kk': 0.85 + nrm((L, R_WIDTH), 0.02),
        'rwkv_ka': 1.0 + nrm((L, R_WIDTH), 0.02),
        'rwkv_rk': nrm((L, R_HEADS, HEAD_DIM), 0.1),
        'rwkv_ln_g': 1.0 + nrm((L, R_WIDTH), 0.02),
        'rwkv_ln_b': nrm((L, R_WIDTH), 0.01),
        'diff_q_gain': 1.0 + nrm((L, DF_QK), 0.02),
        'diff_k_gain': 1.0 + nrm((L, DF_QK), 0.02),
        'diff_lambda': nrm((L, 4, DF_QK), 0.1),
        'diff_subln_g': 1.0 + nrm((L, DF_V), 0.02),
        'moba_q_gain': 1.0 + nrm((L, HEAD_DIM), 0.02),
        'moba_k_gain': 1.0 + nrm((L, HEAD_DIM), 0.02),
        'rel_bias': nrm((NUM_BUCKETS, N_BIAS_HEADS), 0.5),
        'router_g_w': nrm((L, D_MODEL, N_GROUPS), D_MODEL ** -0.5),
        'router_g_b': nrm((L, N_GROUPS), 0.01),
        'router_e_w': nrm((L, D_MODEL, N_EXPERTS), D_MODEL ** -0.5),
        'router_e_b': nrm((L, N_EXPERTS), 0.01),
        'moe_w1': nrm((L, N_EXPERTS, D_MODEL, EXPERT_HIDDEN), D_MODEL ** -0.5),
        'moe_w3': nrm((L, N_EXPERTS, D_MODEL, EXPERT_HIDDEN), D_MODEL ** -0.5),
        'moe_w2': nrm((L, N_EXPERTS, EXPERT_HIDDEN, D_MODEL), EXPERT_HIDDEN ** -0.5),
    }


def reference(x, c, ada_w, ada_b, norm1_g, norm2_g, w_in, w_out, rwkv_mu, rwkv_w0, rwkv_w2,
              rwkv_a0, rwkv_a2, rwkv_g2, rwkv_kk, rwkv_ka, rwkv_rk, rwkv_ln_g, rwkv_ln_b,
              diff_q_gain, diff_k_gain, diff_lambda, diff_subln_g, moba_q_gain, moba_k_gain,
              rel_bias, router_g_w, router_g_b, router_e_w, router_e_b, moe_w1, moe_w3, moe_w2):
    cs = jax.nn.silu(c)
    for l in range(DEPTH):
        mod = cs @ ada_w[l] + ada_b[l]
        sh1, sc1, g1, sh2, sc2, g2 = jnp.split(mod[:, None, :], 6, axis=-1)
        h = rmsnorm(x, norm1_g[l]) * (1 + sc1) + sh1
        proj = h @ w_in[l]
        p_r, p_d, p_m = jnp.split(proj, [R_COLS, R_COLS + DF_COLS], axis=-1)
        y_r = rwkv7_mix(p_r, rwkv_mu[l], rwkv_w0[l], rwkv_w2[l], rwkv_a0[l], rwkv_a2[l],
                        rwkv_g2[l], rwkv_kk[l], rwkv_ka[l], rwkv_rk[l], rwkv_ln_g[l], rwkv_ln_b[l])
        lambda_init = 0.8 - 0.6 * math.exp(-0.3 * l)
        y_d = diff_attention(p_d, diff_q_gain[l], diff_k_gain[l], diff_lambda[l],
                             diff_subln_g[l], rel_bias, lambda_init)
        y_m = moba_attention(p_m, moba_q_gain[l], moba_k_gain[l], rel_bias)
        mix = jnp.concatenate([y_r, y_d, y_m], axis=-1) @ w_out[l]
        x = x + g1 * mix
        h2 = rmsnorm(x, norm2_g[l]) * (1 + sc2) + sh2
        x = x + g2 * hier_moe(h2, router_g_w[l], router_g_b[l], router_e_w[l], router_e_b[l],
                              moe_w1[l], moe_w3[l], moe_w2[l])
    return x
```

```python
import functools
import math

import jax
import jax.numpy as jnp
from jax import lax
from jax.experimental import pallas as pl
from jax.experimental.pallas import tpu as pltpu

F32 = jnp.float32
BF16 = jnp.bfloat16
HIGHEST = lax.Precision.HIGHEST

D_MODEL = 1024
DEPTH = 4
HEAD_DIM = 64
R_HEADS = 4
R_WIDTH = 256
R_LORA_W = 32
R_LORA_A = 32
R_LORA_G = 64
R_COLS = 896
R_GN_EPS = 64e-5
DF_HEADS = 4
DF_V = 128
DF_WIDTH = 512
DF_COLS = 1536
MB_HEADS = 4
MB_WIDTH = 256
MB_COLS = 768
MB_BLOCK = 256
MB_TOPK = 3
IN_COLS = 3200
NUM_BUCKETS = 32
MAX_DISTANCE = 1024
N_GROUPS = 4
EXPERTS_PER_GROUP = 8
N_EXPERTS = 32
TOP_K_INNER = 2
EXPERT_HIDDEN = 512
MOE_BLOCK = 256
RMS_EPS = 1e-6

LANES = 128
ATT_TILE = 256
BIAS_TILES = 6
RW_CHUNK = 64
NEG_BIG = -1e30
VMEM_LIMIT = 48 * 1024 * 1024


def _cparams(sem):
    return pltpu.CompilerParams(dimension_semantics=sem, vmem_limit_bytes=VMEM_LIMIT)


def _mm(a, b):
    return jnp.dot(a.astype(BF16), b.astype(BF16), preferred_element_type=F32)


def _mm_nt(a, b):
    return lax.dot_general(a.astype(BF16), b.astype(BF16), (((1,), (1,)), ((), ())),
                           preferred_element_type=F32)


def _mm_tn(a, b):
    return lax.dot_general(a.astype(BF16), b.astype(BF16), (((0,), (0,)), ((), ())),
                           preferred_element_type=F32)


def _mm_hi(a, b):
    return jnp.dot(a, b, precision=HIGHEST, preferred_element_type=F32)


def _adaln_kernel(c_ref, w_ref, b_ref, o_ref):
    c = c_ref[...]
    cs = c * jax.nn.sigmoid(c)
    o_ref[0] = _mm_hi(cs, w_ref[0]) + b_ref[0]


def _adaln(c, ada_w, ada_b):
    L, D, D6 = ada_w.shape
    B = c.shape[0]
    nj = D6 // D
    return pl.pallas_call(
        _adaln_kernel,
        grid=(L, nj),
        in_specs=[pl.BlockSpec((B, D), lambda l, j: (0, 0)),
                  pl.BlockSpec((1, D, D), lambda l, j: (l, 0, j)),
                  pl.BlockSpec((1, 1, D), lambda l, j: (l, 0, j))],
        out_specs=pl.BlockSpec((1, B, D), lambda l, j: (l, 0, j)),
        out_shape=jax.ShapeDtypeStruct((L, B, D6), F32),
        compiler_params=_cparams(("parallel", "parallel")),
        name="adaln",
    )(c, ada_w, ada_b.reshape(L, 1, D6))


def _ln_inproj_kernel(x_ref, g_ref, sc_ref, sh_ref, w_ref, pr_ref, pd_ref, pm_ref):
    x = x_ref[0]
    y = x * lax.rsqrt(jnp.mean(x * x, -1, keepdims=True) + RMS_EPS)
    h = ((y * g_ref[...]) * (1.0 + sc_ref[0]) + sh_ref[0]).astype(BF16)
    c1 = R_COLS
    c2 = R_COLS + DF_COLS
    pr_ref[0] = jnp.dot(h, w_ref[:, :c1], preferred_element_type=F32)
    pd_ref[0] = jnp.dot(h, w_ref[:, c1:c2], preferred_element_type=F32).astype(BF16)
    pm_ref[0] = jnp.dot(h, w_ref[:, c2:], preferred_element_type=F32).astype(BF16)


def _ln_inproj(x, g, sc, sh, w_bf16, tm=512):
    B, S, D = x.shape
    return pl.pallas_call(
        _ln_inproj_kernel,
        grid=(B, S // tm),
        in_specs=[pl.BlockSpec((1, tm, D), lambda b, i: (b, i, 0)),
                  pl.BlockSpec((1, D), lambda b, i: (0, 0)),
                  pl.BlockSpec((1, 1, D), lambda b, i: (b, 0, 0)),
                  pl.BlockSpec((1, 1, D), lambda b, i: (b, 0, 0)),
                  pl.BlockSpec((D, IN_COLS), lambda b, i: (0, 0))],
        out_specs=[pl.BlockSpec((1, tm, R_COLS), lambda b, i: (b, i, 0)),
                   pl.BlockSpec((1, tm, DF_COLS), lambda b, i: (b, i, 0)),
                   pl.BlockSpec((1, tm, MB_COLS), lambda b, i: (b, i, 0))],
        out_shape=[jax.ShapeDtypeStruct((B, S, R_COLS), F32),
                   jax.ShapeDtypeStruct((B, S, DF_COLS), BF16),
                   jax.ShapeDtypeStruct((B, S, MB_COLS), BF16)],
        compiler_params=_cparams(("parallel", "parallel")),
        name="ln_inproj",
    )(x, g.reshape(1, D), sc.reshape(B, 1, D), sh.reshape(B, 1, D), w_bf16)


def _head_ones(n):
    r = lax.broadcasted_iota(jnp.int32, (n, n), 0) // HEAD_DIM
    c = lax.broadcasted_iota(jnp.int32, (n, n), 1) // HEAD_DIM
    return (r == c).astype(F32)


def _rwkv_prep_kernel(p_ref, pp_ref, mu_ref, w0_ref, w2_ref, a0_ref, a2_ref, g2_ref, kk_ref, ka_ref,
                      rk_ref, r_o, k_o, v_o, kap_o, b_o, lw_o, g_o, bonus_o):
    i = pl.program_id(1)
    p = p_ref[0]
    tm = p.shape[0]
    prev_row = jnp.where(i > 0, pp_ref[0][7:8, :], 0.0)
    rows = lax.broadcasted_iota(jnp.int32, p.shape, 0)
    p_prev = jnp.where(rows == 0, prev_row, pltpu.roll(p, 1, 0))
    ps = p + (p_prev - p) * mu_ref[...]
    W = R_WIDTH
    r = ps[:, 0:W]
    k = ps[:, W:2 * W]
    v = ps[:, 2 * W:3 * W]
    c3 = 3 * W
    c4 = c3 + R_LORA_W
    c5 = c4 + R_LORA_A
    wd = ps[:, c3:c4]
    ad = ps[:, c4:c5]
    gd = ps[:, c5:R_COLS]
    z = -(w0_ref[...] + _mm_hi(jnp.tanh(wd), w2_ref[...]))
    softplus = jnp.maximum(z, 0.0) + jnp.log(1.0 + jnp.exp(-jnp.abs(z)))
    lw_o[0] = -jnp.exp(-softplus - 0.5)
    a = jax.nn.sigmoid(a0_ref[...] + _mm_hi(ad, a2_ref[...]))
    g_o[0] = _mm_hi(jax.nn.sigmoid(gd), g2_ref[...])
    ones = _head_ones(W)
    kk = k * kk_ref[...]
    nrm = jnp.sqrt(_mm_hi(kk * kk, ones))
    kap = kk / jnp.maximum(nrm, 1e-12)
    k2 = k * (1.0 + (a - 1.0) * ka_ref[...])
    r_o[0] = r
    k_o[0] = k2
    v_o[0] = v
    kap_o[0] = kap
    b_o[0] = kap * a
    bonus_o[0] = _mm_hi(r * k2 * rk_ref[...], ones) * v


def _rwkv_prep(p_r, mu, w0, w2, a0, a2, g2, k_k, k_a, r_k, tm=512):
    B, S, _ = p_r.shape
    W = R_WIDTH
    row = lambda a: a.reshape(1, -1)
    full = lambda a: pl.BlockSpec(a.shape, lambda b, i: (0,) * a.ndim)
    ins = [row(mu), row(w0), w2, row(a0), a2, g2, row(k_k), row(k_a), row(r_k)]
    out_spec = pl.BlockSpec((1, tm, W), lambda b, i: (b, i, 0))
    return pl.pallas_call(
        _rwkv_prep_kernel,
        grid=(B, S // tm),
        in_specs=[pl.BlockSpec((1, tm, R_COLS), lambda b, i: (b, i, 0)),
                  pl.BlockSpec((1, 8, R_COLS), lambda b, i: (b, jnp.maximum(i * (tm // 8) - 1, 0), 0))]
                 + [full(a) for a in ins],
        out_specs=[out_spec] * 8,
        out_shape=[jax.ShapeDtypeStruct((B, S, W), F32)] * 8,
        compiler_params=_cparams(("parallel", "arbitrary")),
        name="rwkv_prep",
    )(p_r, p_r, *ins)


def _rwkv_chunk_kernel(r_ref, k_ref, v_ref, kap_ref, b_ref, lw_ref, g_ref, bonus_ref, lng_ref, lnb_ref,
                       o_ref, state):
    c = pl.program_id(1)
    C = RW_CHUNK
    Dh = HEAD_DIM

    @pl.when(c == 0)
    def _():
        state[...] = jnp.zeros_like(state)

    ti = lax.broadcasted_iota(jnp.int32, (C, C), 0)
    tj = lax.broadcasted_iota(jnp.int32, (C, C), 1)
    low_incl = tj <= ti
    low_strict = tj < ti
    eye = (ti == tj).astype(F32)

    lw = lw_ref[0]
    cum = _mm_hi(low_incl.astype(F32), lw)
    cum_end = cum[C - 1:C, :]
    e_pos = jnp.exp(cum)
    e_neg = jnp.exp(-cum)
    e_prev = jnp.exp(cum - lw)
    e_end = jnp.exp(cum_end - cum)
    g_end = jnp.exp(cum_end)
    r = r_ref[0]
    k = k_ref[0]
    v = v_ref[0]
    b = b_ref[0]
    rt = r * e_pos
    kapt = kap_ref[0] * e_prev
    bt = b * e_neg
    kt = k * e_neg
    bh = b * e_end
    kh = k * e_end
    gate = g_ref[0]
    bonus = bonus_ref[0]
    lng = lng_ref[...]
    lnb = lnb_ref[...]

    lane = lax.broadcasted_iota(jnp.int32, (Dh, R_WIDTH), 1)
    sub = lax.broadcasted_iota(jnp.int32, (Dh, R_WIDTH), 0)
    out = jnp.zeros((C, R_WIDTH), F32)
    for h in range(R_HEADS):
        sl = slice(h * Dh, (h + 1) * Dh)
        kap_h, rt_h, bt_h, kt_h, bh_h, kh_h, v_h = (kapt[:, sl], rt[:, sl], bt[:, sl], kt[:, sl],
                                                     bh[:, sl], kh[:, sl], v[:, sl])
        l_b = jnp.where(low_strict, _mm_nt(kap_h, bt_h), 0.0)
        l_k = jnp.where(low_strict, _mm_nt(kap_h, kt_h), 0.0)
        m_b = jnp.where(low_incl, _mm_nt(rt_h, bt_h), 0.0)
        m_k = jnp.where(low_incl, _mm_nt(rt_h, kt_h), 0.0)
        xp = -l_b
        tinv = eye + xp
        for _ in range(5):
            xp = _mm(xp, xp)
            tinv = tinv + _mm(tinv, xp)
        w_h = _mm(tinv, kap_h)
        p_loc = -_mm(tinv, _mm(l_k, v_h))
        q_eff = rt_h - _mm(m_b, w_h)
        y_loc = _mm(m_b, p_loc) + _mm(m_k, v_h)
        diag_g = eye * g_end[:, sl]
        g_mat = diag_g - _mm_tn(w_h, bh_h)
        u_mat = _mm_tn(p_loc, bh_h) + _mm_tn(v_h, kh_h)
        s_old = state[h]
        y = _mm_nt(q_eff, s_old) + y_loc
        state[h] = _mm(s_old, g_mat) + u_mat
        mean = jnp.mean(y, -1, keepdims=True)
        yc = y - mean
        var = jnp.mean(yc * yc, -1, keepdims=True)
        yn = yc * lax.rsqrt(var + R_GN_EPS)
        yfin = (yn * lng[:, sl] + lnb[:, sl] + bonus[:, sl]) * gate[:, sl]
        place = (lane == sub + h * Dh).astype(BF16)
        out = out + jnp.dot(yfin.astype(BF16), place, preferred_element_type=F32)
    o_ref[0] = out.astype(BF16)


def _rwkv_chunk(r, k, v, kap, b, lw, g, bonus, ln_g, ln_b):
    B, S, W = r.shape
    C = RW_CHUNK
    spec = pl.BlockSpec((1, C, W), lambda bb, c: (bb, c, 0))
    vec = pl.BlockSpec((1, W), lambda bb, c: (0, 0))
    return pl.pallas_call(
        _rwkv_chunk_kernel,
        grid=(B, S // C),
        in_specs=[spec] * 8 + [vec, vec],
        out_specs=spec,
        out_shape=jax.ShapeDtypeStruct((B, S, W), BF16),
        scratch_shapes=[pltpu.VMEM((R_HEADS, HEAD_DIM, HEAD_DIM), F32)],
        compiler_params=_cparams(("parallel", "arbitrary")),
        name="rwkv_chunk",
    )(r, k, v, kap, b, lw, g, bonus, ln_g.reshape(1, W), ln_b.reshape(1, W))


def _t5_bucket(dist):
    n = jnp.maximum(dist, 0)
    max_exact = NUM_BUCKETS // 2
    nf = jnp.maximum(n, 1).astype(F32)
    large = max_exact + (jnp.log(nf / max_exact) / math.log(MAX_DISTANCE / max_exact)
                         * (NUM_BUCKETS - max_exact)).astype(jnp.int32)
    large = jnp.minimum(large, NUM_BUCKETS - 1)
    return jnp.where(n < max_exact, n, large)


def _bias_tiles(rel_bias, seq):
    T = ATT_TILE
    assert (BIAS_TILES - 2) * T + 1 >= MAX_DISTANCE and seq % T == 0
    o = jnp.arange(BIAS_TILES)[:, None, None]
    i = jnp.arange(T)[None, :, None]
    j = jnp.arange(T)[None, None, :]
    dist = o * T + i - j
    tiles = rel_bias.astype(F32)[_t5_bucket(dist)]
    tiles = jnp.where((dist >= 0)[..., None], tiles, NEG_BIG)
    return tiles.transpose(3, 0, 1, 2)


def _half_masks(shape):
    lane = lax.broadcasted_iota(jnp.int32, shape, len(shape) - 1)
    lo = lane < HEAD_DIM
    return lo, jnp.logical_not(lo)


def _rms_halves(x, gain2):
    lo, hi = _half_masks(x.shape)
    sq = x * x
    s_lo = jnp.sum(jnp.where(lo, sq, 0.0), -1, keepdims=True)
    s_hi = jnp.sum(jnp.where(hi, sq, 0.0), -1, keepdims=True)
    inv = jnp.where(lo, lax.rsqrt(s_lo / HEAD_DIM + RMS_EPS), lax.rsqrt(s_hi / HEAD_DIM + RMS_EPS))
    return x * inv * gain2


def _diff_attn_kernel(q_ref, k_ref, v_ref, bias_ref, qg_ref, kg_ref, lam_ref, sg_ref, o_ref,
                      kn, m_s, l_s, acc_s, *, lambda_init):
    qi = pl.program_id(2)
    T = ATT_TILE
    nkb = k_ref.shape[1] // T

    @pl.when(qi == 0)
    def _():
        def body(j, carry):
            rows = pl.ds(pl.multiple_of(j * T, T), T)
            kn[rows, :] = _rms_halves(k_ref[0, rows, :].astype(F32), kg_ref[...]).astype(BF16)
            return carry
        lax.fori_loop(0, nkb, body, 0)

    q = _rms_halves(q_ref[0].astype(F32), qg_ref[...]) * (HEAD_DIM ** -0.5)
    lo, hi = _half_masks(q.shape)
    q2 = jnp.concatenate([jnp.where(lo, q, 0.0), jnp.where(hi, q, 0.0)], axis=0).astype(BF16)
    m_s[...] = jnp.full_like(m_s, -jnp.inf)
    l_s[...] = jnp.zeros_like(l_s)
    acc_s[...] = jnp.zeros_like(acc_s)

    def body(j, carry):
        rows = pl.ds(pl.multiple_of(j * T, T), T)
        kb = kn[rows, :]
        vb = v_ref[0, rows, :]
        bias = bias_ref[0, jnp.minimum(qi - j, BIAS_TILES - 1)]
        s = lax.dot_general(q2, kb, (((1,), (1,)), ((), ())), preferred_element_type=F32)
        s = s + jnp.concatenate([bias, bias], axis=0)
        m_old = m_s[...]
        m_new = jnp.maximum(m_old, jnp.max(s, -1, keepdims=True))
        alpha = jnp.exp(m_old - m_new)
        p = jnp.exp(s - m_new)
        l_s[...] = alpha * l_s[...] + jnp.sum(p, -1, keepdims=True)
        acc_s[...] = alpha * acc_s[...] + jnp.dot(p.astype(BF16), vb, preferred_element_type=F32)
        m_s[...] = m_new
        return carry

    lax.fori_loop(0, qi + 1, body, 0)

    lam = lam_ref[...]
    lam_full = (jnp.exp(jnp.sum(lam[0:1] * lam[1:2], -1, keepdims=True))
                - jnp.exp(jnp.sum(lam[2:3] * lam[3:4], -1, keepdims=True)) + lambda_init)
    o = acc_s[...] / l_s[...]
    out = o[:T] - lam_full * o[T:]
    out = out * lax.rsqrt(jnp.mean(out * out, -1, keepdims=True) + RMS_EPS) * sg_ref[...]
    o_ref[0] = (out * (1.0 - lambda_init)).astype(BF16)


def _diff_attn(p_d, bias_tiles, q_gain, k_gain, lam, subln_g, lambda_init):
    B, S, _ = p_d.shape
    T = ATT_TILE
    H = DF_HEADS
    gain2 = lambda g: jnp.concatenate([g, g]).reshape(1, 2 * HEAD_DIM)
    kern = functools.partial(_diff_attn_kernel, lambda_init=lambda_init)
    return pl.pallas_call(
        kern,
        grid=(B, H, S // T),
        in_specs=[pl.BlockSpec((1, T, LANES), lambda b, h, i: (b, i, h)),
                  pl.BlockSpec((1, S, LANES), lambda b, h, i: (b, 0, H + h)),
                  pl.BlockSpec((1, S, LANES), lambda b, h, i: (b, 0, 2 * H + h)),
                  pl.BlockSpec((1, BIAS_TILES, T, T), lambda b, h, i: (h, 0, 0, 0)),
                  pl.BlockSpec((1, LANES), lambda b, h, i: (0, 0)),
                  pl.BlockSpec((1, LANES), lambda b, h, i: (0, 0)),
                  pl.BlockSpec((4, HEAD_DIM), lambda b, h, i: (0, 0)),
                  pl.BlockSpec((1, DF_V), lambda b, h, i: (0, 0))],
        out_specs=pl.BlockSpec((1, T, DF_V), lambda b, h, i: (b, i, h)),
        out_shape=jax.ShapeDtypeStruct((B, S, DF_WIDTH), BF16),
        scratch_shapes=[pltpu.VMEM((S, LANES), BF16),
                        pltpu.VMEM((2 * T, 1), F32),
                        pltpu.VMEM((2 * T, 1), F32),
                        pltpu.VMEM((2 * T, DF_V), F32)],
        compiler_params=_cparams(("parallel", "parallel", "arbitrary")),
        name="diff_attn",
    )(p_d, p_d, p_d, bias_tiles, gain2(q_gain), gain2(k_gain), lam, subln_g.reshape(1, DF_V))


def _moba_kernel(q_ref, k_ref, v_ref, bias_ref, qg_ref, kg_ref, o_ref, kn, kmean, m_s, l_s, acc_s):
    qi = pl.program_id(2)
    T = ATT_TILE
    nkb = k_ref.shape[1] // T

    @pl.when(qi == 0)
    def _():
        def body(j, carry):
            rows = pl.ds(pl.multiple_of(j * T, T), T)
            kf = _rms_halves(k_ref[0, rows, :].astype(F32), kg_ref[...])
            kn[rows, :] = kf.astype(BF16)
            kmean[pl.ds(j, 1), :] = jnp.mean(kf, 0, keepdims=True)
            return carry
        lax.fori_loop(0, nkb, body, 0)

    q = _rms_halves(q_ref[0].astype(F32), qg_ref[...])
    lo, hi = _half_masks(q.shape)
    q2f = jnp.concatenate([jnp.where(lo, q, 0.0), jnp.where(hi, q, 0.0)], axis=0)
    q2 = (q2f * (HEAD_DIM ** -0.5)).astype(BF16)

    gate = lax.dot_general(q2f, kmean[...], (((1,), (1,)), ((), ())), precision=HIGHEST,
                           preferred_element_type=F32)
    blk = lax.broadcasted_iota(jnp.int32, gate.shape, 1).astype(F32)
    gate = jnp.where(blk < qi.astype(F32), gate, -jnp.inf)
    pen = jnp.full(gate.shape, NEG_BIG, F32)
    for _ in range(MB_TOPK):
        mx = jnp.max(gate, -1, keepdims=True)
        first = jnp.min(jnp.where(gate == mx, blk, float(nkb)), -1, keepdims=True)
        pick = jnp.logical_and(blk == first, mx > -jnp.inf)
        pen = jnp.where(pick, 0.0, pen)
        gate = jnp.where(pick, -jnp.inf, gate)

    m_s[...] = jnp.full_like(m_s, -jnp.inf)
    l_s[...] = jnp.zeros_like(l_s)
    acc_s[...] = jnp.zeros_like(acc_s)

    def step(j, extra):
        rows = pl.ds(pl.multiple_of(j * T, T), T)
        kb = kn[rows, :]
        vb = v_ref[0, rows, :]
        o_idx = jnp.minimum(qi - j, BIAS_TILES - 1)
        s = lax.dot_general(q2, kb, (((1,), (1,)), ((), ())), preferred_element_type=F32)
        s = s + jnp.concatenate([bias_ref[0, o_idx], bias_ref[1, o_idx]], axis=0) + extra
        m_old = m_s[...]
        m_new = jnp.maximum(m_old, jnp.max(s, -1, keepdims=True))
        alpha = jnp.exp(m_old - m_new)
        p = jnp.exp(s - m_new)
        l_s[...] = alpha * l_s[...] + jnp.sum(p, -1, keepdims=True)
        acc_s[...] = alpha * acc_s[...] + jnp.dot(p.astype(BF16), vb, preferred_element_type=F32)
        m_s[...] = m_new

    def body(j, carry):
        col = jnp.sum(jnp.where(blk == j.astype(F32), pen, 0.0), -1, keepdims=True)
        step(j, col)
        return carry

    lax.fori_loop(0, qi, body, 0)
    step(qi, 0.0)

    o = acc_s[...] / l_s[...]
    lo_o, _ = _half_masks((T, LANES))
    o_ref[0] = jnp.where(lo_o, o[:T], o[T:]).astype(BF16)


def _moba(p_m, bias_tiles, q_gain, k_gain):
    B, S, _ = p_m.shape
    T = ATT_TILE
    HP = MB_HEADS // 2
    nkb = S // T
    gain2 = lambda g: jnp.concatenate([g, g]).reshape(1, 2 * HEAD_DIM)
    return pl.pallas_call(
        _moba_kernel,
        grid=(B, HP, S // T),
        in_specs=[pl.BlockSpec((1, T, LANES), lambda b, h, i: (b, i, h)),
                  pl.BlockSpec((1, S, LANES), lambda b, h, i: (b, 0, HP + h)),
                  pl.BlockSpec((1, S, LANES), lambda b, h, i: (b, 0, 2 * HP + h)),
                  pl.BlockSpec((2, BIAS_TILES, T, T), lambda b, h, i: (h, 0, 0, 0)),
                  pl.BlockSpec((1, LANES), lambda b, h, i: (0, 0)),
                  pl.BlockSpec((1, LANES), lambda b, h, i: (0, 0))],
        out_specs=pl.BlockSpec((1, T, LANES), lambda b, h, i: (b, i, h)),
        out_shape=jax.ShapeDtypeStruct((B, S, MB_WIDTH), BF16),
        scratch_shapes=[pltpu.VMEM((S, LANES), BF16),
                        pltpu.VMEM((nkb, LANES), F32),
                        pltpu.VMEM((2 * T, 1), F32),
                        pltpu.VMEM((2 * T, 1), F32),
                        pltpu.VMEM((2 * T, LANES), F32)],
        compiler_params=_cparams(("parallel", "parallel", "arbitrary")),
        name="moba",
    )(p_m, p_m, p_m, bias_tiles, gain2(q_gain), gain2(k_gain))


def _pack_pairs(x):
    n = x.shape[1] // 2
    hi = pltpu.bitcast(x[:, :n].astype(BF16).astype(F32), jnp.uint32)
    lo = pltpu.bitcast(x[:, n:].astype(BF16).astype(F32), jnp.uint32)
    return hi | (lo >> 16)


def _unpack_pairs(p):
    hi = pltpu.bitcast(p & jnp.uint32(0xFFFF0000), F32)
    lo = pltpu.bitcast(p << 16, F32)
    return jnp.concatenate([hi, lo], axis=1)


def _outproj_kernel(yr_ref, yd_ref, ym_ref, x_ref, g1_ref, sc_ref, sh_ref, ng_ref, w_ref, wr_ref, br_ref,
                    x1_ref, h2_ref, route_ref):
    c1 = R_WIDTH
    c2 = R_WIDTH + DF_WIDTH
    mix = (jnp.dot(yr_ref[0], w_ref[:c1, :], preferred_element_type=F32)
           + jnp.dot(yd_ref[0], w_ref[c1:c2, :], preferred_element_type=F32)
           + jnp.dot(ym_ref[0], w_ref[c2:, :], preferred_element_type=F32))
    x1 = x_ref[0] + g1_ref[0] * mix
    x1_ref[0] = x1
    y = x1 * lax.rsqrt(jnp.mean(x1 * x1, -1, keepdims=True) + RMS_EPS)
    h2 = (y * ng_ref[...]) * (1.0 + sc_ref[0]) + sh_ref[0]
    h2_ref[0] = _pack_pairs(h2)

    logits = _mm_hi(h2, wr_ref[...]) + br_ref[...]
    lane = lax.broadcasted_iota(jnp.int32, logits.shape, 1).astype(F32)
    far = float(LANES)
    is_g = lane < N_GROUPS
    gl = jnp.where(is_g, logits, -jnp.inf)
    gmax = jnp.max(gl, -1, keepdims=True)
    g_idx = jnp.min(jnp.where(gl == gmax, lane, far), -1, keepdims=True)
    pg_top = 1.0 / jnp.sum(jnp.where(is_g, jnp.exp(logits - gmax), 0.0), -1, keepdims=True)
    e_lo = N_GROUPS + EXPERTS_PER_GROUP * g_idx
    in_grp = jnp.logical_and(lane >= e_lo, lane < e_lo + EXPERTS_PER_GROUP)
    el = jnp.where(in_grp, logits, -jnp.inf)
    m1 = jnp.max(el, -1, keepdims=True)
    i1 = jnp.min(jnp.where(el == m1, lane, far), -1, keepdims=True)
    el2 = jnp.where(lane == i1, -jnp.inf, el)
    m2 = jnp.max(el2, -1, keepdims=True)
    i2 = jnp.min(jnp.where(el2 == m2, lane, far), -1, keepdims=True)
    e2 = jnp.exp(m2 - m1)
    gate1 = pg_top / (1.0 + e2)
    gate2 = gate1 * e2
    route = jnp.where(lane == 0, gate1, 0.0)
    route = jnp.where(lane == 1, gate2, route)
    route = jnp.where(lane == 2, i1 - N_GROUPS, route)
    route = jnp.where(lane == 3, i2 - N_GROUPS, route)
    route_ref[0] = route


def _outproj(y_r, y_d, y_m, x, g1, sc2, sh2, norm2_g, w_out_bf16, w_router, b_router, tm=512):
    B, S, D = x.shape
    vec = pl.BlockSpec((1, 1, D), lambda b, i: (b, 0, 0))
    tok = lambda w: pl.BlockSpec((1, tm, w), lambda b, i: (b, i, 0))
    full = lambda a: pl.BlockSpec(a.shape, lambda b, i: (0,) * a.ndim)
    return pl.pallas_call(
        _outproj_kernel,
        grid=(B, S // tm),
        in_specs=[tok(R_WIDTH), tok(DF_WIDTH), tok(MB_WIDTH), tok(D), vec, vec, vec,
                  pl.BlockSpec((1, D), lambda b, i: (0, 0)), full(w_out_bf16), full(w_router), full(b_router)],
        out_specs=[tok(D), tok(D // 2), tok(LANES)],
        out_shape=[jax.ShapeDtypeStruct((B, S, D), F32),
                   jax.ShapeDtypeStruct((B, S, D // 2), jnp.uint32),
                   jax.ShapeDtypeStruct((B, S, LANES), F32)],
        compiler_params=_cparams(("parallel", "parallel")),
        name="outproj_router",
    )(y_r, y_d, y_m, x, g1.reshape(B, 1, D), sc2.reshape(B, 1, D), sh2.reshape(B, 1, D),
      norm2_g.reshape(1, D), w_out_bf16, w_router, b_router)


def _rank_kernel(route_ref, rank_ref, cnt_ref, carry):
    i = pl.program_id(0)

    @pl.when(i == 0)
    def _():
        carry[...] = jnp.zeros_like(carry)

    rt = route_ref[...]
    tm = rt.shape[0]
    lane = lax.broadcasted_iota(jnp.int32, rt.shape, 1).astype(F32)
    oh0 = (lane == rt[:, 2:3]).astype(F32)
    oh1 = (lane == rt[:, 3:4]).astype(F32)
    oh = oh0 + oh1
    row = lax.broadcasted_iota(jnp.int32, (tm, tm), 0)
    col = lax.broadcasted_iota(jnp.int32, (tm, tm), 1)
    before = jnp.dot((col < row).astype(BF16), oh.astype(BF16), preferred_element_type=F32) + carry[...]
    r0 = jnp.sum(oh0 * before, -1, keepdims=True)
    r1 = jnp.sum(oh1 * before, -1, keepdims=True)
    rank_ref[...] = jnp.where(lane == 0, r0, jnp.where(lane == 1, r1, 0.0))
    carry[...] = carry[...] + jnp.sum(oh, 0, keepdims=True)
    cnt_ref[...] = carry[...]


def _ranks(route, tm=512):
    T = route.shape[0]
    return pl.pallas_call(
        _rank_kernel,
        grid=(T // tm,),
        in_specs=[pl.BlockSpec((tm, LANES), lambda i: (i, 0))],
        out_specs=[pl.BlockSpec((tm, LANES), lambda i: (i, 0)), pl.BlockSpec((1, LANES), lambda i: (0, 0))],
        out_shape=[jax.ShapeDtypeStruct((T, LANES), F32), jax.ShapeDtypeStruct((1, LANES), F32)],
        scratch_shapes=[pltpu.VMEM((1, LANES), F32)],
        compiler_params=_cparams(("arbitrary",)),
        name="moe_ranks",
    )(route)


DMA_UNROLL = 8


def _dispatch_kernel(dest_ref, h_ref, buf_in, buf_out, sem):
    del buf_in
    tm = h_ref.shape[0]

    def row_copy(t, slot):
        return pltpu.make_async_copy(h_ref.at[pl.ds(t, 1)], buf_out.at[pl.ds(slot, 1)], sem)

    def issue(t, carry):
        for k in range(TOP_K_INNER):
            row_copy(t, dest_ref[TOP_K_INNER * t + k]).start()
        return carry

    def drain(t, carry):
        for k in range(TOP_K_INNER):
            row_copy(0, 0).wait()
        return carry

    lax.fori_loop(0, tm, issue, 0, unroll=DMA_UNROLL)
    lax.fori_loop(0, tm, drain, 0, unroll=DMA_UNROLL)


def _dispatch(h2p, dest_flat, n_slots, tm=512):
    T, W = h2p.shape
    buf0 = jnp.zeros((n_slots, W), jnp.uint32)
    return pl.pallas_call(
        _dispatch_kernel,
        grid=(T // tm,),
        in_specs=[pl.BlockSpec((TOP_K_INNER * tm,), lambda i: (i,), memory_space=pltpu.SMEM),
                  pl.BlockSpec((tm, W), lambda i: (i, 0)),
                  pl.BlockSpec(memory_space=pl.ANY)],
        out_specs=pl.BlockSpec(memory_space=pl.ANY),
        out_shape=jax.ShapeDtypeStruct((n_slots, W), jnp.uint32),
        scratch_shapes=[pltpu.SemaphoreType.DMA],
        input_output_aliases={2: 0},
        compiler_params=_cparams(("arbitrary",)),
        name="moe_dispatch",
    )(dest_flat, h2p, buf0)


def _expert_kernel(be_ref, nu_ref, x_ref, w1_ref, w3_ref, w2_ref, o_ref):
    del be_ref
    i = pl.program_id(0)

    @pl.when(i < nu_ref[0])
    def _():
        x = _unpack_pairs(x_ref[...]).astype(BF16)
        h1 = jnp.dot(x, w1_ref[0].astype(BF16), preferred_element_type=F32)
        h3 = jnp.dot(x, w3_ref[0].astype(BF16), preferred_element_type=F32)
        h = (h1 * jax.nn.sigmoid(h1) * h3).astype(BF16)
        o_ref[...] = _pack_pairs(jnp.dot(h, w2_ref[0].astype(BF16), preferred_element_type=F32))

    @pl.when(i >= nu_ref[0])
    def _():
        o_ref[...] = jnp.zeros_like(o_ref)


def _experts(buf, blk_e, n_used, w1, w3, w2):
    n_slots, W = buf.shape
    E, D, Hd = w1.shape
    nb = n_slots // MOE_BLOCK
    grid_spec = pltpu.PrefetchScalarGridSpec(
        num_scalar_prefetch=2,
        grid=(nb,),
        in_specs=[pl.BlockSpec((MOE_BLOCK, W), lambda i, be, nu: (i, 0)),
                  pl.BlockSpec((1, D, Hd), lambda i, be, nu: (be[i], 0, 0)),
                  pl.BlockSpec((1, D, Hd), lambda i, be, nu: (be[i], 0, 0)),
                  pl.BlockSpec((1, Hd, D), lambda i, be, nu: (be[i], 0, 0))],
        out_specs=pl.BlockSpec((MOE_BLOCK, W), lambda i, be, nu: (i, 0)),
    )
    return pl.pallas_call(
        _expert_kernel,
        grid_spec=grid_spec,
        out_shape=jax.ShapeDtypeStruct((n_slots, W), jnp.uint32),
        compiler_params=_cparams(("arbitrary",)),
        name="moe_experts",
    )(blk_e, n_used, buf, w1, w3, w2)


def _combine_kernel(dest_ref, yb_ref, route_ref, x_ref, g2_ref, o_ref, rows, sem):
    tm = x_ref.shape[0]

    def row_copy(t, k, slot):
        return pltpu.make_async_copy(yb_ref.at[pl.ds(slot, 1)], rows.at[k, pl.ds(t, 1)], sem)

    def issue(t, carry):
        for k in range(TOP_K_INNER):
            row_copy(t, k, dest_ref[TOP_K_INNER * t + k]).start()
        return carry

    def drain(t, carry):
        for k in range(TOP_K_INNER):
            row_copy(0, k, 0).wait()
        return carry

    lax.fori_loop(0, tm, issue, 0, unroll=DMA_UNROLL)
    lax.fori_loop(0, tm, drain, 0, unroll=DMA_UNROLL)
    rt = route_ref[...]
    y = rt[:, 0:1] * _unpack_pairs(rows[0]) + rt[:, 1:2] * _unpack_pairs(rows[1])
    o_ref[...] = x_ref[...] + g2_ref[0] * y


def _combine(yb, dest_flat, route, x1, g2, seq, tm=256):
    T, D = x1.shape
    W = yb.shape[1]
    per = seq // tm
    B = T // seq
    return pl.pallas_call(
        _combine_kernel,
        grid=(T // tm,),
        in_specs=[pl.BlockSpec((TOP_K_INNER * tm,), lambda i: (i,), memory_space=pltpu.SMEM),
                  pl.BlockSpec(memory_space=pl.ANY),
                  pl.BlockSpec((tm, LANES), lambda i: (i, 0)),
                  pl.BlockSpec((tm, D), lambda i: (i, 0)),
                  pl.BlockSpec((1, 1, D), lambda i: (i // per, 0, 0))],
        out_specs=pl.BlockSpec((tm, D), lambda i: (i, 0)),
        out_shape=jax.ShapeDtypeStruct((T, D), F32),
        scratch_shapes=[pltpu.VMEM((TOP_K_INNER, tm, W), jnp.uint32), pltpu.SemaphoreType.DMA],
        compiler_params=_cparams(("arbitrary",)),
        name="moe_combine",
    )(dest_flat, yb, route, x1, g2.reshape(B, 1, D))


def _moe(h2p, route, x1, g2, w1, w3, w2, seq):
    T, D = x1.shape
    A = T * TOP_K_INNER
    n_blocks = -(-(A + N_EXPERTS * (MOE_BLOCK - 1)) // MOE_BLOCK)
    rank, cnt = _ranks(route)
    counts = cnt[0, :N_EXPERTS].astype(jnp.int32)
    padded = (counts + MOE_BLOCK - 1) // MOE_BLOCK * MOE_BLOCK
    pad_end = jnp.cumsum(padded)
    pad_start = pad_end - padded
    experts = route[:, 2:2 + TOP_K_INNER].astype(jnp.int32)
    dest = (pad_start[experts] + rank[:, :TOP_K_INNER].astype(jnp.int32)).reshape(A)
    blk_e = jnp.minimum(jnp.searchsorted(pad_end, jnp.arange(n_blocks) * MOE_BLOCK, side='right'),
                        N_EXPERTS - 1).astype(jnp.int32)
    n_used = (pad_end[-1:] // MOE_BLOCK).astype(jnp.int32)
    buf = _dispatch(h2p, dest, n_blocks * MOE_BLOCK)
    yb = _experts(buf, blk_e, n_used, w1, w3, w2)
    return _combine(yb, dest, route, x1, g2, seq)


def kernel(x, c, ada_w, ada_b, norm1_g, norm2_g, w_in, w_out, rwkv_mu, rwkv_w0, rwkv_w2, rwkv_a0, rwkv_a2, rwkv_g2, rwkv_kk, rwkv_ka, rwkv_rk, rwkv_ln_g, rwkv_ln_b, diff_q_gain, diff_k_gain, diff_lambda, diff_subln_g, moba_q_gain, moba_k_gain, rel_bias, router_g_w, router_g_b, router_e_w, router_e_b, moe_w1, moe_w3, moe_w2):
    B, S, D = x.shape
    T = B * S
    mods = _adaln(c, ada_w, ada_b)
    bias = _bias_tiles(rel_bias, S)
    n_route = N_GROUPS + N_EXPERTS
    for l in range(DEPTH):
        sh1, sc1, g1, sh2, sc2, g2 = [mods[l, :, j * D:(j + 1) * D] for j in range(6)]
        p_r, p_d, p_m = _ln_inproj(x, norm1_g[l], sc1, sh1, w_in[l].astype(BF16))
        r, k, v, kap, b, lw, g, bonus = _rwkv_prep(p_r, rwkv_mu[l], rwkv_w0[l], rwkv_w2[l], rwkv_a0[l],
                                                   rwkv_a2[l], rwkv_g2[l], rwkv_kk[l], rwkv_ka[l],
                                                   rwkv_rk[l])
        y_r = _rwkv_chunk(r, k, v, kap, b, lw, g, bonus, rwkv_ln_g[l], rwkv_ln_b[l])
        lambda_init = 0.8 - 0.6 * math.exp(-0.3 * l)
        y_d = _diff_attn(p_d, bias[:DF_HEADS], diff_q_gain[l], diff_k_gain[l], diff_lambda[l],
                         diff_subln_g[l], lambda_init)
        y_m = _moba(p_m, bias[DF_HEADS:], moba_q_gain[l], moba_k_gain[l])
        w_router = jnp.zeros((D, LANES), F32).at[:, :n_route].set(
            jnp.concatenate([router_g_w[l], router_e_w[l]], axis=1))
        b_router = jnp.zeros((1, LANES), F32).at[0, :n_route].set(
            jnp.concatenate([router_g_b[l], router_e_b[l]]))
        x1, h2p, route = _outproj(y_r, y_d, y_m, x, g1, sc2, sh2, norm2_g[l], w_out[l].astype(BF16),
                                  w_router, b_router)
        x2 = _moe(h2p.reshape(T, D // 2), route.reshape(T, LANES), x1.reshape(T, D), g2,
                  moe_w1[l], moe_w3[l], moe_w2[l], S)
        x = x2.reshape(B, S, D)
    return x
```

```python
import functools
import math

import jax
import jax.numpy as jnp
from jax import lax
from jax.experimental import pallas as pl
from jax.experimental.pallas import tpu as pltpu

F32 = jnp.float32
BF16 = jnp.bfloat16
HIGHEST = lax.Precision.HIGHEST

D_MODEL = 1024
DEPTH = 4
HEAD_DIM = 64
R_HEADS = 4
R_WIDTH = 256
R_LORA_W = 32
R_LORA_A = 32
R_LORA_G = 64
R_COLS = 896
R_GN_EPS = 64e-5
DF_HEADS = 4
DF_V = 128
DF_WIDTH = 512
DF_COLS = 1536
MB_HEADS = 4
MB_WIDTH = 256
MB_COLS = 768
MB_BLOCK = 256
MB_TOPK = 3
IN_COLS = 3200
NUM_BUCKETS = 32
MAX_DISTANCE = 1024
N_GROUPS = 4
EXPERTS_PER_GROUP = 8
N_EXPERTS = 32
TOP_K_INNER = 2
EXPERT_HIDDEN = 512
MOE_BLOCK = 256
RMS_EPS = 1e-6

LANES = 128
ATT_TILE = 256
SAT_TILE = 5
MASK_TILE = 6
N_BIAS_TILES = 7
ATT_UNROLL = 2
RW_CHUNK = 64
RW_CHUNKS_PER_STEP = 4
NEG_BIG = -1e30
LOG2E = math.log2(math.e)
QK_SCALE = HEAD_DIM ** -0.5 * LOG2E
VMEM_LIMIT = 48 * 1024 * 1024


def _cparams(sem):
    return pltpu.CompilerParams(dimension_semantics=sem, vmem_limit_bytes=VMEM_LIMIT)


def _mm(a, b):
    return jnp.dot(a.astype(BF16), b.astype(BF16), preferred_element_type=F32)


def _mm_nt(a, b):
    return lax.dot_general(a.astype(BF16), b.astype(BF16), (((1,), (1,)), ((), ())),
                           preferred_element_type=F32)


def _mm_tn(a, b):
    return lax.dot_general(a.astype(BF16), b.astype(BF16), (((0,), (0,)), ((), ())),
                           preferred_element_type=F32)


def _mm_hi(a, b):
    return jnp.dot(a, b, precision=HIGHEST, preferred_element_type=F32)


def _adaln_kernel(c_ref, w_ref, b_ref, o_ref):
    c = c_ref[...]
    cs = c * jax.nn.sigmoid(c)
    o_ref[0] = _mm_hi(cs, w_ref[0]) + b_ref[0]


def _adaln(c, ada_w, ada_b):
    L, D, D6 = ada_w.shape
    B = c.shape[0]
    nj = D6 // D
    return pl.pallas_call(
        _adaln_kernel,
        grid=(L, nj),
        in_specs=[pl.BlockSpec((B, D), lambda l, j: (0, 0)),
                  pl.BlockSpec((1, D, D), lambda l, j: (l, 0, j)),
                  pl.BlockSpec((1, 1, D), lambda l, j: (l, 0, j))],
        out_specs=pl.BlockSpec((1, B, D), lambda l, j: (l, 0, j)),
        out_shape=jax.ShapeDtypeStruct((L, B, D6), F32),
        compiler_params=_cparams(("parallel", "parallel")),
        name="adaln",
    )(c, ada_w, ada_b.reshape(L, 1, D6))


def _ln_inproj_kernel(x_ref, g_ref, sc_ref, sh_ref, w_ref, pr_ref, pd_ref, pm_ref):
    x = x_ref[0]
    y = x * lax.rsqrt(jnp.mean(x * x, -1, keepdims=True) + RMS_EPS)
    h = ((y * g_ref[...]) * (1.0 + sc_ref[0]) + sh_ref[0]).astype(BF16)
    c1 = R_COLS
    c2 = R_COLS + DF_COLS
    pr_ref[0] = jnp.dot(h, w_ref[:, :c1], preferred_element_type=F32)
    pd_ref[0] = jnp.dot(h, w_ref[:, c1:c2], preferred_element_type=F32).astype(BF16)
    pm_ref[0] = jnp.dot(h, w_ref[:, c2:], preferred_element_type=F32).astype(BF16)


def _ln_inproj(x, g, sc, sh, w_bf16, tm=512):
    B, S, D = x.shape
    return pl.pallas_call(
        _ln_inproj_kernel,
        grid=(B, S // tm),
        in_specs=[pl.BlockSpec((1, tm, D), lambda b, i: (b, i, 0)),
                  pl.BlockSpec((1, D), lambda b, i: (0, 0)),
                  pl.BlockSpec((1, 1, D), lambda b, i: (b, 0, 0)),
                  pl.BlockSpec((1, 1, D), lambda b, i: (b, 0, 0)),
                  pl.BlockSpec((D, IN_COLS), lambda b, i: (0, 0))],
        out_specs=[pl.BlockSpec((1, tm, R_COLS), lambda b, i: (b, i, 0)),
                   pl.BlockSpec((1, tm, DF_COLS), lambda b, i: (b, i, 0)),
                   pl.BlockSpec((1, tm, MB_COLS), lambda b, i: (b, i, 0))],
        out_shape=[jax.ShapeDtypeStruct((B, S, R_COLS), F32),
                   jax.ShapeDtypeStruct((B, S, DF_COLS), BF16),
                   jax.ShapeDtypeStruct((B, S, MB_COLS), BF16)],
        compiler_params=_cparams(("parallel", "parallel")),
        name="ln_inproj",
    )(x, g.reshape(1, D), sc.reshape(B, 1, D), sh.reshape(B, 1, D), w_bf16)


def _head_ones(n):
    r = lax.broadcasted_iota(jnp.int32, (n, n), 0) // HEAD_DIM
    c = lax.broadcasted_iota(jnp.int32, (n, n), 1) // HEAD_DIM
    return (r == c).astype(F32)


def _rwkv_prep_kernel(p_ref, pp_ref, mu_ref, w0_ref, w2_ref, a0_ref, a2_ref, g2_ref, kk_ref, ka_ref,
                      rk_ref, r_o, k_o, v_o, kap_o, b_o, lw_o, g_o, bonus_o):
    i = pl.program_id(1)
    p = p_ref[0]
    prev_row = jnp.where(i > 0, pp_ref[0][7:8, :], 0.0)
    rows = lax.broadcasted_iota(jnp.int32, p.shape, 0)
    p_prev = jnp.where(rows == 0, prev_row, pltpu.roll(p, 1, 0))
    ps = p + (p_prev - p) * mu_ref[...]
    W = R_WIDTH
    r = ps[:, 0:W]
    k = ps[:, W:2 * W]
    v = ps[:, 2 * W:3 * W]
    c3 = 3 * W
    c4 = c3 + R_LORA_W
    c5 = c4 + R_LORA_A
    wd = ps[:, c3:c4]
    ad = ps[:, c4:c5]
    gd = ps[:, c5:R_COLS]
    z = -(w0_ref[...] + _mm_hi(jnp.tanh(wd), w2_ref[...]))
    softplus = jnp.maximum(z, 0.0) + jnp.log(1.0 + jnp.exp(-jnp.abs(z)))
    lw_o[0] = -jnp.exp(-softplus - 0.5)
    a = jax.nn.sigmoid(a0_ref[...] + _mm_hi(ad, a2_ref[...]))
    g_o[0] = _mm_hi(jax.nn.sigmoid(gd), g2_ref[...])
    ones = _head_ones(W)
    kk = k * kk_ref[...]
    nrm = jnp.sqrt(_mm_hi(kk * kk, ones))
    kap = kk / jnp.maximum(nrm, 1e-12)
    k2 = k * (1.0 + (a - 1.0) * ka_ref[...])
    r_o[0] = r
    k_o[0] = k2
    v_o[0] = v
    kap_o[0] = kap
    b_o[0] = kap * a
    bonus_o[0] = _mm_hi(r * k2 * rk_ref[...], ones) * v


def _rwkv_prep(p_r, mu, w0, w2, a0, a2, g2, k_k, k_a, r_k, tm=512):
    B, S, _ = p_r.shape
    W = R_WIDTH
    row = lambda a: a.reshape(1, -1)
    full = lambda a: pl.BlockSpec(a.shape, lambda b, i: (0,) * a.ndim)
    ins = [row(mu), row(w0), w2, row(a0), a2, g2, row(k_k), row(k_a), row(r_k)]
    out_spec = pl.BlockSpec((1, tm, W), lambda b, i: (b, i, 0))
    return pl.pallas_call(
        _rwkv_prep_kernel,
        grid=(B, S // tm),
        in_specs=[pl.BlockSpec((1, tm, R_COLS), lambda b, i: (b, i, 0)),
                  pl.BlockSpec((1, 8, R_COLS), lambda b, i: (b, jnp.maximum(i * (tm // 8) - 1, 0), 0))]
                 + [full(a) for a in ins],
        out_specs=[out_spec] * 8,
        out_shape=[jax.ShapeDtypeStruct((B, S, W), F32)] * 8,
        compiler_params=_cparams(("parallel", "arbitrary")),
        name="rwkv_prep",
    )(p_r, p_r, *ins)


def _rwkv_chunk_kernel(r_ref, k_ref, v_ref, kap_ref, b_ref, lw_ref, g_ref, bonus_ref, lng_ref, lnb_ref,
                       o_ref, state):
    step = pl.program_id(1)
    C = RW_CHUNK
    NC = RW_CHUNKS_PER_STEP
    Dh = HEAD_DIM
    TM = NC * C

    @pl.when(step == 0)
    def _():
        state[...] = jnp.zeros_like(state)

    ti = lax.broadcasted_iota(jnp.int32, (C, C), 0)
    tj = lax.broadcasted_iota(jnp.int32, (C, C), 1)
    eye = (ti == tj).astype(F32)
    ti2 = lax.broadcasted_iota(jnp.int32, (C, 2 * C), 0)
    tj2 = lax.broadcasted_iota(jnp.int32, (C, 2 * C), 1) % C
    strict2 = tj2 < ti2
    incl2 = tj2 <= ti2

    bi = lax.broadcasted_iota(jnp.int32, (TM, TM), 0)
    bj = lax.broadcasted_iota(jnp.int32, (TM, TM), 1)
    chunk_tri = jnp.logical_and(bi // C == bj // C, bj <= bi).astype(F32)

    lw = lw_ref[0]
    cum = _mm_hi(chunk_tri, lw)
    cum_end = jnp.concatenate(
        [jnp.broadcast_to(cum[(c + 1) * C - 1:(c + 1) * C, :], (C, R_WIDTH)) for c in range(NC)], axis=0)
    e_pos = jnp.exp(cum)
    e_neg = jnp.exp(-cum)
    e_prev = jnp.exp(cum - lw)
    e_end = jnp.exp(cum_end - cum)
    g_end = jnp.exp(cum_end)
    r = r_ref[0]
    k = k_ref[0]
    v = v_ref[0]
    b = b_ref[0]
    rt = r * e_pos
    kapt = kap_ref[0] * e_prev
    bt = b * e_neg
    kt = k * e_neg
    bh = b * e_end
    kh = k * e_end

    items = [(c, h) for c in range(NC) for h in range(R_HEADS)]

    def tile(x, c, h):
        return x[c * C:(c + 1) * C, h * Dh:(h + 1) * Dh]

    a_top, a_bot = [], []
    for c, h in items:
        lhs = jnp.concatenate([tile(kapt, c, h), tile(rt, c, h)], axis=0)
        rhs = jnp.concatenate([tile(bt, c, h), tile(kt, c, h)], axis=0)
        a = _mm_nt(lhs, rhs)
        a_top.append(jnp.where(strict2, a[:C], 0.0))
        a_bot.append(jnp.where(incl2, a[C:], 0.0))
    xp = [-t[:, :C] for t in a_top]
    tinv = [eye + x for x in xp]
    xp = [_mm(x, x) for x in xp]
    for _ in range(4):
        prod = [_mm(jnp.concatenate([t, x], axis=0), x) for t, x in zip(tinv, xp)]
        tinv = [t + p[:C] for t, p in zip(tinv, prod)]
        xp = [p[C:] for p in prod]
    tinv = [t + _mm(t, x) for t, x in zip(tinv, xp)]
    vt = [tile(v, c, h) for c, h in items]
    lkv = [_mm(t[:, C:], vv) for t, vv in zip(a_top, vt)]
    w = [_mm(t, tile(kapt, c, h)) for t, (c, h) in zip(tinv, items)]
    ploc = [-_mm(t, x) for t, x in zip(tinv, lkv)]
    q_eff = [tile(rt, c, h) - _mm(ab[:, :C], ww) for ab, ww, (c, h) in zip(a_bot, w, items)]
    pv = [jnp.concatenate([p, vv], axis=0) for p, vv in zip(ploc, vt)]
    y_loc = [_mm(ab, x) for ab, x in zip(a_bot, pv)]
    g_mat = [eye * tile(g_end, c, h)[0:1, :] - _mm_tn(ww, tile(bh, c, h)) for ww, (c, h) in zip(w, items)]
    u_mat = [_mm_tn(x, jnp.concatenate([tile(bh, c, h), tile(kh, c, h)], axis=0))
             for x, (c, h) in zip(pv, items)]

    s_cur = [state[h] for h in range(R_HEADS)]
    ys = [[None] * NC for _ in range(R_HEADS)]
    for n, (c, h) in enumerate(items):
        ys[h][c] = _mm_nt(q_eff[n], s_cur[h]) + y_loc[n]
        s_cur[h] = _mm(s_cur[h], g_mat[n]) + u_mat[n]
    for h in range(R_HEADS):
        state[h] = s_cur[h]

    gate = g_ref[0]
    bonus = bonus_ref[0]
    lng = lng_ref[...]
    lnb = lnb_ref[...]
    lane = lax.broadcasted_iota(jnp.int32, (Dh, R_WIDTH), 1)
    sub = lax.broadcasted_iota(jnp.int32, (Dh, R_WIDTH), 0)
    out = jnp.zeros((TM, R_WIDTH), F32)
    for h in range(R_HEADS):
        sl = slice(h * Dh, (h + 1) * Dh)
        y = jnp.concatenate(ys[h], axis=0)
        mean = jnp.mean(y, -1, keepdims=True)
        yc = y - mean
        var = jnp.mean(yc * yc, -1, keepdims=True)
        yn = yc * lax.rsqrt(var + R_GN_EPS)
        yfin = (yn * lng[:, sl] + lnb[:, sl] + bonus[:, sl]) * gate[:, sl]
        place = (lane == sub + h * Dh).astype(BF16)
        out = out + jnp.dot(yfin.astype(BF16), place, preferred_element_type=F32)
    o_ref[0] = out.astype(BF16)


def _rwkv_chunk(r, k, v, kap, b, lw, g, bonus, ln_g, ln_b):
    B, S, W = r.shape
    tm = RW_CHUNK * RW_CHUNKS_PER_STEP
    spec = pl.BlockSpec((1, tm, W), lambda bb, c: (bb, c, 0))
    vec = pl.BlockSpec((1, W), lambda bb, c: (0, 0))
    return pl.pallas_call(
        _rwkv_chunk_kernel,
        grid=(B, S // tm),
        in_specs=[spec] * 8 + [vec, vec],
        out_specs=spec,
        out_shape=jax.ShapeDtypeStruct((B, S, W), BF16),
        scratch_shapes=[pltpu.VMEM((R_HEADS, HEAD_DIM, HEAD_DIM), F32)],
        compiler_params=_cparams(("parallel", "arbitrary")),
        name="rwkv_chunk",
    )(r, k, v, kap, b, lw, g, bonus, ln_g.reshape(1, W), ln_b.reshape(1, W))


def _t5_bucket(dist):
    n = jnp.maximum(dist, 0)
    max_exact = NUM_BUCKETS // 2
    nf = jnp.maximum(n, 1).astype(F32)
    large = max_exact + (jnp.log(nf / max_exact) / math.log(MAX_DISTANCE / max_exact)
                         * (NUM_BUCKETS - max_exact)).astype(jnp.int32)
    large = jnp.minimum(large, NUM_BUCKETS - 1)
    return jnp.where(n < max_exact, n, large)


def _bias_tiles(tbl):
    T = ATT_TILE
    assert (SAT_TILE - 1) * T + 1 >= MAX_DISTANCE
    nh = tbl.shape[1]
    o = jnp.arange(SAT_TILE + 1)[:, None, None]
    i = jnp.arange(T)[None, :, None]
    j = jnp.arange(T)[None, None, :]
    dist = o * T + i - j
    onehot = (_t5_bucket(dist)[None] == jnp.arange(NUM_BUCKETS)[:, None, None, None]).astype(F32)
    tiles = jnp.dot(tbl.astype(F32).T, onehot.reshape(NUM_BUCKETS, -1), precision=HIGHEST)
    tiles = jnp.where((dist >= 0)[None], tiles.reshape(nh, SAT_TILE + 1, T, T) * LOG2E, NEG_BIG)
    return jnp.concatenate([tiles, jnp.full((nh, 1, T, T), NEG_BIG, F32)], axis=1)


def _half_masks(shape):
    lane = lax.broadcasted_iota(jnp.int32, shape, len(shape) - 1)
    lo = lane < HEAD_DIM
    return lo, jnp.logical_not(lo)


def _rms_halves(x, gain2):
    lo, hi = _half_masks(x.shape)
    sq = x * x
    s_lo = jnp.sum(jnp.where(lo, sq, 0.0), -1, keepdims=True)
    s_hi = jnp.sum(jnp.where(hi, sq, 0.0), -1, keepdims=True)
    inv = jnp.where(lo, lax.rsqrt(s_lo / HEAD_DIM + RMS_EPS), lax.rsqrt(s_hi / HEAD_DIM + RMS_EPS))
    return x * inv * gain2


def _lane_chunks(x):
    return [x[:, c * LANES:(c + 1) * LANES] for c in range(x.shape[1] // LANES)]


def _flash(q2, kn, v_ref, bias_tile, qi, s_bufs, m_s, l_s, acc_s):
    T = ATT_TILE
    U = ATT_UNROLL
    n_iter = (qi + U) // U

    def window(i):
        return pl.ds(pl.multiple_of(jnp.minimum(i, n_iter - 1) * (U * T), U * T), U * T)

    def produce(i, dst):
        s = lax.dot_general(q2, kn[window(i), :], (((1,), (1,)), ((), ())), preferred_element_type=F32)
        tiles = []
        for u in range(U):
            off = qi - (i * U + u)
            tiles.append(bias_tile(jnp.where(off < 0, MASK_TILE, jnp.minimum(off, SAT_TILE))))
        dst[...] = s + jnp.concatenate(tiles, axis=1)

    def consume(i, src):
        parts = _lane_chunks(src[...])
        m_old = m_s[...]
        blk_max = jnp.max(functools.reduce(jnp.maximum, parts), -1, keepdims=True)
        m_new = jnp.maximum(m_old, jnp.broadcast_to(blk_max, m_old.shape))
        alpha = jnp.exp2(m_old - m_new)
        ps = [jnp.exp2(part - m_new) for part in parts]
        l_s[...] = alpha * l_s[...] + functools.reduce(lambda a, b: a + b, ps)
        p = jnp.concatenate(ps, axis=1).astype(BF16)
        acc_s[...] = alpha * acc_s[...] + jnp.dot(p, v_ref[0, window(i), :], preferred_element_type=F32)
        m_s[...] = m_new

    m_s[...] = jnp.full_like(m_s, -jnp.inf)
    l_s[...] = jnp.zeros_like(l_s)
    acc_s[...] = jnp.zeros_like(acc_s)
    produce(0, s_bufs[0])

    def body(i, carry):
        for par in range(2):
            @pl.when(i % 2 == par)
            def _():
                produce(i + 1, s_bufs[1 - par])
                consume(i, s_bufs[par])
        return carry

    lax.fori_loop(0, n_iter, body, 0)
    return acc_s[...] / jnp.sum(l_s[...], -1, keepdims=True)


def _diff_attn_kernel(q_ref, k_ref, v_ref, bias_ref, qg_ref, kg_ref, lam_ref, sg_ref, o_ref,
                      kn, s_a, s_b, m_s, l_s, acc_s, *, lambda_init):
    qi = pl.program_id(2)
    T = ATT_TILE
    nkb = k_ref.shape[1] // T

    @pl.when(qi == 0)
    def _():
        def body(j, carry):
            rows = pl.ds(pl.multiple_of(j * T, T), T)
            kn[rows, :] = _rms_halves(k_ref[0, rows, :].astype(F32), kg_ref[...]).astype(BF16)
            return carry
        lax.fori_loop(0, nkb, body, 0)

    q = _rms_halves(q_ref[0].astype(F32), qg_ref[...]) * QK_SCALE
    lo, hi = _half_masks(q.shape)
    q2 = jnp.concatenate([jnp.where(lo, q, 0.0), jnp.where(hi, q, 0.0)], axis=0).astype(BF16)

    def bias_tile(idx):
        t = bias_ref[0, idx]
        return jnp.concatenate([t, t], axis=0)

    o = _flash(q2, kn, v_ref, bias_tile, qi, (s_a, s_b), m_s, l_s, acc_s)
    lam = lam_ref[...]
    lam_full = (jnp.exp(jnp.sum(lam[0:1] * lam[1:2], -1, keepdims=True))
                - jnp.exp(jnp.sum(lam[2:3] * lam[3:4], -1, keepdims=True)) + lambda_init)
    out = o[:T] - lam_full * o[T:]
    out = out * lax.rsqrt(jnp.mean(out * out, -1, keepdims=True) + RMS_EPS) * sg_ref[...]
    o_ref[0] = (out * (1.0 - lambda_init)).astype(BF16)


def _diff_attn(p_d, bias_tiles, q_gain, k_gain, lam, subln_g, lambda_init):
    B, S, _ = p_d.shape
    T = ATT_TILE
    H = DF_HEADS
    assert S % (ATT_UNROLL * T) == 0
    gain2 = lambda g: jnp.concatenate([g, g]).reshape(1, 2 * HEAD_DIM)
    kern = functools.partial(_diff_attn_kernel, lambda_init=lambda_init)
    return pl.pallas_call(
        kern,
        grid=(B, H, S // T),
        in_specs=[pl.BlockSpec((1, T, LANES), lambda b, h, i: (b, i, h)),
                  pl.BlockSpec((1, S, LANES), lambda b, h, i: (b, 0, H + h)),
                  pl.BlockSpec((1, S, LANES), lambda b, h, i: (b, 0, 2 * H + h)),
                  pl.BlockSpec((1, N_BIAS_TILES, T, T), lambda b, h, i: (h, 0, 0, 0)),
                  pl.BlockSpec((1, LANES), lambda b, h, i: (0, 0)),
                  pl.BlockSpec((1, LANES), lambda b, h, i: (0, 0)),
                  pl.BlockSpec((4, HEAD_DIM), lambda b, h, i: (0, 0)),
                  pl.BlockSpec((1, DF_V), lambda b, h, i: (0, 0))],
        out_specs=pl.BlockSpec((1, T, DF_V), lambda b, h, i: (b, i, h)),
        out_shape=jax.ShapeDtypeStruct((B, S, DF_WIDTH), BF16),
        scratch_shapes=[pltpu.VMEM((S, LANES), BF16),
                        pltpu.VMEM((2 * T, ATT_UNROLL * T), F32),
                        pltpu.VMEM((2 * T, ATT_UNROLL * T), F32),
                        pltpu.VMEM((2 * T, LANES), F32),
                        pltpu.VMEM((2 * T, LANES), F32),
                        pltpu.VMEM((2 * T, DF_V), F32)],
        compiler_params=_cparams(("parallel", "parallel", "arbitrary")),
        name="diff_attn",
    )(p_d, p_d, p_d, bias_tiles, gain2(q_gain), gain2(k_gain), lam, subln_g.reshape(1, DF_V))


def _moba_kernel(q_ref, k_ref, v_ref, bias_ref, qg_ref, kg_ref, o_ref, kn, kmean, s_a, s_b, m_s, l_s,
                 acc_s):
    qi = pl.program_id(2)
    T = ATT_TILE
    nkb = k_ref.shape[1] // T

    @pl.when(qi == 0)
    def _():
        kmean[...] = jnp.zeros_like(kmean)
        lane = lax.broadcasted_iota(jnp.int32, (T, LANES), 1)

        def body(j, carry):
            rows = pl.ds(pl.multiple_of(j * T, T), T)
            kf = _rms_halves(k_ref[0, rows, :].astype(F32), kg_ref[...])
            kn[rows, :] = jnp.concatenate([kf.astype(BF16), (lane == j).astype(BF16)], axis=1)
            kmean[pl.ds(j, 1), :] = jnp.mean(kf, 0, keepdims=True)
            return carry
        lax.fori_loop(0, nkb, body, 0)

    q = _rms_halves(q_ref[0].astype(F32), qg_ref[...])
    lo, hi = _half_masks(q.shape)
    q2f = jnp.concatenate([jnp.where(lo, q, 0.0), jnp.where(hi, q, 0.0)], axis=0)

    nkp = kmean.shape[0]
    km = kmean[...]
    km_hi = km.astype(BF16)
    km_lo = (km - km_hi.astype(F32)).astype(BF16)
    q_hi = q2f.astype(BF16)
    q_lo = (q2f - q_hi.astype(F32)).astype(BF16)
    nt = (((1,), (1,)), ((), ()))
    g_hi = lax.dot_general(jnp.concatenate([km_hi, km_lo], axis=0), q_hi, nt, preferred_element_type=F32)
    gate = g_hi[:nkp] + g_hi[nkp:] + lax.dot_general(km_hi, q_lo, nt, preferred_element_type=F32)
    blk = lax.broadcasted_iota(jnp.int32, gate.shape, 0).astype(F32)
    past = blk < qi.astype(F32)
    gate = jnp.where(past, gate, -jnp.inf)
    pen_t = jnp.where(past, NEG_BIG, 0.0)
    for _ in range(MB_TOPK):
        mx = jnp.max(gate, 0, keepdims=True)
        first = jnp.min(jnp.where(gate == mx, blk, float(nkp)), 0, keepdims=True)
        pick = jnp.logical_and(blk == first, mx > -jnp.inf)
        pen_t = jnp.where(pick, 0.0, pen_t)
        gate = jnp.where(pick, -jnp.inf, gate)
    pen = jnp.concatenate([pen_t, jnp.zeros((LANES - nkp, 2 * T), F32)], axis=0).T
    q2 = jnp.concatenate([(q2f * QK_SCALE).astype(BF16), pen.astype(BF16)], axis=1)

    def bias_tile(idx):
        return jnp.concatenate([bias_ref[0, idx], bias_ref[1, idx]], axis=0)

    o = _flash(q2, kn, v_ref, bias_tile, qi, (s_a, s_b), m_s, l_s, acc_s)
    lo_o, _ = _half_masks((T, LANES))
    o_ref[0] = jnp.where(lo_o, o[:T], o[T:]).astype(BF16)


def _moba(p_m, bias_tiles, q_gain, k_gain):
    B, S, _ = p_m.shape
    T = ATT_TILE
    HP = MB_HEADS // 2
    assert T == MB_BLOCK and S // T <= LANES and S % (ATT_UNROLL * T) == 0
    gain2 = lambda g: jnp.concatenate([g, g]).reshape(1, 2 * HEAD_DIM)
    return pl.pallas_call(
        _moba_kernel,
        grid=(B, HP, S // T),
        in_specs=[pl.BlockSpec((1, T, LANES), lambda b, h, i: (b, i, h)),
                  pl.BlockSpec((1, S, LANES), lambda b, h, i: (b, 0, HP + h)),
                  pl.BlockSpec((1, S, LANES), lambda b, h, i: (b, 0, 2 * HP + h)),
                  pl.BlockSpec((2, N_BIAS_TILES, T, T), lambda b, h, i: (h, 0, 0, 0)),
                  pl.BlockSpec((1, LANES), lambda b, h, i: (0, 0)),
                  pl.BlockSpec((1, LANES), lambda b, h, i: (0, 0))],
        out_specs=pl.BlockSpec((1, T, LANES), lambda b, h, i: (b, i, h)),
        out_shape=jax.ShapeDtypeStruct((B, S, MB_WIDTH), BF16),
        scratch_shapes=[pltpu.VMEM((S, 2 * LANES), BF16),
                        pltpu.VMEM((-(-(S // T) // 8) * 8, LANES), F32),
                        pltpu.VMEM((2 * T, ATT_UNROLL * T), F32),
                        pltpu.VMEM((2 * T, ATT_UNROLL * T), F32),
                        pltpu.VMEM((2 * T, LANES), F32),
                        pltpu.VMEM((2 * T, LANES), F32),
                        pltpu.VMEM((2 * T, LANES), F32)],
        compiler_params=_cparams(("parallel", "parallel", "arbitrary")),
        name="moba",
    )(p_m, p_m, p_m, bias_tiles, gain2(q_gain), gain2(k_gain))


def _pack_pairs(x):
    n = x.shape[1] // 2
    hi = pltpu.bitcast(x[:, :n].astype(BF16).astype(F32), jnp.uint32)
    lo = pltpu.bitcast(x[:, n:].astype(BF16).astype(F32), jnp.uint32)
    return hi | (lo >> 16)


def _unpack_pairs(p):
    hi = pltpu.bitcast(p & jnp.uint32(0xFFFF0000), F32)
    lo = pltpu.bitcast(p << 16, F32)
    return jnp.concatenate([hi, lo], axis=1)


def _outproj_kernel(yr_ref, yd_ref, ym_ref, x_ref, g1_ref, sc_ref, sh_ref, ng_ref, w_ref, wr_ref, br_ref,
                    x1_ref, h2_ref, route_ref):
    c1 = R_WIDTH
    c2 = R_WIDTH + DF_WIDTH
    mix = (jnp.dot(yr_ref[0], w_ref[:c1, :], preferred_element_type=F32)
           + jnp.dot(yd_ref[0], w_ref[c1:c2, :], preferred_element_type=F32)
           + jnp.dot(ym_ref[0], w_ref[c2:, :], preferred_element_type=F32))
    x1 = x_ref[0] + g1_ref[0] * mix
    x1_ref[0] = x1
    y = x1 * lax.rsqrt(jnp.mean(x1 * x1, -1, keepdims=True) + RMS_EPS)
    h2 = (y * ng_ref[...]) * (1.0 + sc_ref[0]) + sh_ref[0]
    h2_ref[0] = _pack_pairs(h2)

    logits = _mm_hi(h2, wr_ref[...]) + br_ref[...]
    lane = lax.broadcasted_iota(jnp.int32, logits.shape, 1).astype(F32)
    far = float(LANES)
    is_g = lane < N_GROUPS
    gl = jnp.where(is_g, logits, -jnp.inf)
    gmax = jnp.max(gl, -1, keepdims=True)
    g_idx = jnp.min(jnp.where(gl == gmax, lane, far), -1, keepdims=True)
    pg_top = 1.0 / jnp.sum(jnp.where(is_g, jnp.exp(logits - gmax), 0.0), -1, keepdims=True)
    e_lo = N_GROUPS + EXPERTS_PER_GROUP * g_idx
    in_grp = jnp.logical_and(lane >= e_lo, lane < e_lo + EXPERTS_PER_GROUP)
    el = jnp.where(in_grp, logits, -jnp.inf)
    m1 = jnp.max(el, -1, keepdims=True)
    i1 = jnp.min(jnp.where(el == m1, lane, far), -1, keepdims=True)
    el2 = jnp.where(lane == i1, -jnp.inf, el)
    m2 = jnp.max(el2, -1, keepdims=True)
    i2 = jnp.min(jnp.where(el2 == m2, lane, far), -1, keepdims=True)
    e2 = jnp.exp(m2 - m1)
    gate1 = pg_top / (1.0 + e2)
    gate2 = gate1 * e2
    route = jnp.where(lane == 0, gate1, 0.0)
    route = jnp.where(lane == 1, gate2, route)
    route = jnp.where(lane == 2, i1 - N_GROUPS, route)
    route = jnp.where(lane == 3, i2 - N_GROUPS, route)
    route_ref[0] = route


def _outproj(y_r, y_d, y_m, x, g1, sc2, sh2, norm2_g, w_out_bf16, w_router, b_router, tm=512):
    B, S, D = x.shape
    vec = pl.BlockSpec((1, 1, D), lambda b, i: (b, 0, 0))
    tok = lambda w: pl.BlockSpec((1, tm, w), lambda b, i: (b, i, 0))
    full = lambda a: pl.BlockSpec(a.shape, lambda b, i: (0,) * a.ndim)
    return pl.pallas_call(
        _outproj_kernel,
        grid=(B, S // tm),
        in_specs=[tok(R_WIDTH), tok(DF_WIDTH), tok(MB_WIDTH), tok(D), vec, vec, vec,
                  pl.BlockSpec((1, D), lambda b, i: (0, 0)), full(w_out_bf16), full(w_router), full(b_router)],
        out_specs=[tok(D), tok(D // 2), tok(LANES)],
        out_shape=[jax.ShapeDtypeStruct((B, S, D), F32),
                   jax.ShapeDtypeStruct((B, S, D // 2), jnp.uint32),
                   jax.ShapeDtypeStruct((B, S, LANES), F32)],
        compiler_params=_cparams(("parallel", "parallel")),
        name="outproj_router",
    )(y_r, y_d, y_m, x, g1.reshape(B, 1, D), sc2.reshape(B, 1, D), sh2.reshape(B, 1, D),
      norm2_g.reshape(1, D), w_out_bf16, w_router, b_router)


def _rank_kernel(route_ref, rank_ref, cnt_ref, carry):
    i = pl.program_id(0)

    @pl.when(i == 0)
    def _():
        carry[...] = jnp.zeros_like(carry)

    rt = route_ref[...]
    tm = rt.shape[0]
    lane = lax.broadcasted_iota(jnp.int32, rt.shape, 1).astype(F32)
    oh0 = (lane == rt[:, 2:3]).astype(F32)
    oh1 = (lane == rt[:, 3:4]).astype(F32)
    oh = oh0 + oh1
    row = lax.broadcasted_iota(jnp.int32, (tm, tm), 0)
    col = lax.broadcasted_iota(jnp.int32, (tm, tm), 1)
    before = jnp.dot((col < row).astype(BF16), oh.astype(BF16), preferred_element_type=F32) + carry[...]
    r0 = jnp.sum(oh0 * before, -1, keepdims=True)
    r1 = jnp.sum(oh1 * before, -1, keepdims=True)
    rank_ref[...] = jnp.where(lane == 0, r0, jnp.where(lane == 1, r1, 0.0))
    carry[...] = carry[...] + jnp.sum(oh, 0, keepdims=True)
    cnt_ref[...] = carry[...]


def _ranks(route, tm=512):
    T = route.shape[0]
    return pl.pallas_call(
        _rank_kernel,
        grid=(T // tm,),
        in_specs=[pl.BlockSpec((tm, LANES), lambda i: (i, 0))],
        out_specs=[pl.BlockSpec((tm, LANES), lambda i: (i, 0)), pl.BlockSpec((1, LANES), lambda i: (0, 0))],
        out_shape=[jax.ShapeDtypeStruct((T, LANES), F32), jax.ShapeDtypeStruct((1, LANES), F32)],
        scratch_shapes=[pltpu.VMEM((1, LANES), F32)],
        compiler_params=_cparams(("arbitrary",)),
        name="moe_ranks",
    )(route)


DMA_UNROLL = 8


def _dispatch_kernel(dest_ref, h_ref, buf_in, buf_out, sem):
    del buf_in
    tm = h_ref.shape[0]

    def row_copy(t, slot):
        return pltpu.make_async_copy(h_ref.at[pl.ds(t, 1)], buf_out.at[pl.ds(slot, 1)], sem)

    def issue(t, carry):
        for k in range(TOP_K_INNER):
            row_copy(t, dest_ref[TOP_K_INNER * t + k]).start()
        return carry

    def drain(t, carry):
        for k in range(TOP_K_INNER):
            row_copy(0, 0).wait()
        return carry

    lax.fori_loop(0, tm, issue, 0, unroll=DMA_UNROLL)
    lax.fori_loop(0, tm, drain, 0, unroll=DMA_UNROLL)


def _dispatch(h2p, dest_flat, n_slots, tm=512):
    T, W = h2p.shape
    buf0 = jnp.zeros((n_slots, W), jnp.uint32)
    return pl.pallas_call(
        _dispatch_kernel,
        grid=(T // tm,),
        in_specs=[pl.BlockSpec((TOP_K_INNER * tm,), lambda i: (i,), memory_space=pltpu.SMEM),
                  pl.BlockSpec((tm, W), lambda i: (i, 0)),
                  pl.BlockSpec(memory_space=pl.ANY)],
        out_specs=pl.BlockSpec(memory_space=pl.ANY),
        out_shape=jax.ShapeDtypeStruct((n_slots, W), jnp.uint32),
        scratch_shapes=[pltpu.SemaphoreType.DMA],
        input_output_aliases={2: 0},
        compiler_params=_cparams(("arbitrary",)),
        name="moe_dispatch",
    )(dest_flat, h2p, buf0)


def _expert_kernel(be_ref, nu_ref, x_ref, w1_ref, w3_ref, w2_ref, o_ref):
    del be_ref
    i = pl.program_id(0)

    @pl.when(i < nu_ref[0])
    def _():
        x = _unpack_pairs(x_ref[...]).astype(BF16)
        h1 = jnp.dot(x, w1_ref[0].astype(BF16), preferred_element_type=F32)
        h3 = jnp.dot(x, w3_ref[0].astype(BF16), preferred_element_type=F32)
        h = (h1 * jax.nn.sigmoid(h1) * h3).astype(BF16)
        o_ref[...] = _pack_pairs(jnp.dot(h, w2_ref[0].astype(BF16), preferred_element_type=F32))

    @pl.when(i >= nu_ref[0])
    def _():
        o_ref[...] = jnp.zeros_like(o_ref)


def _experts(buf, blk_e, n_used, w1, w3, w2):
    n_slots, W = buf.shape
    E, D, Hd = w1.shape
    nb = n_slots // MOE_BLOCK
    grid_spec = pltpu.PrefetchScalarGridSpec(
        num_scalar_prefetch=2,
        grid=(nb,),
        in_specs=[pl.BlockSpec((MOE_BLOCK, W), lambda i, be, nu: (i, 0)),
                  pl.BlockSpec((1, D, Hd), lambda i, be, nu: (be[i], 0, 0)),
                  pl.BlockSpec((1, D, Hd), lambda i, be, nu: (be[i], 0, 0)),
                  pl.BlockSpec((1, Hd, D), lambda i, be, nu: (be[i], 0, 0))],
        out_specs=pl.BlockSpec((MOE_BLOCK, W), lambda i, be, nu: (i, 0)),
    )
    return pl.pallas_call(
        _expert_kernel,
        grid_spec=grid_spec,
        out_shape=jax.ShapeDtypeStruct((n_slots, W), jnp.uint32),
        compiler_params=_cparams(("arbitrary",)),
        name="moe_experts",
    )(blk_e, n_used, buf, w1, w3, w2)


def _combine_kernel(dest_ref, yb_ref, route_ref, x_ref, g2_ref, o_ref, rows, sem):
    tm = x_ref.shape[0]

    def row_copy(t, k, slot):
        return pltpu.make_async_copy(yb_ref.at[pl.ds(slot, 1)], rows.at[k, pl.ds(t, 1)], sem)

    def issue(t, carry):
        for k in range(TOP_K_INNER):
            row_copy(t, k, dest_ref[TOP_K_INNER * t + k]).start()
        return carry

    def drain(t, carry):
        for k in range(TOP_K_INNER):
            row_copy(0, k, 0).wait()
        return carry

    lax.fori_loop(0, tm, issue, 0, unroll=DMA_UNROLL)
    lax.fori_loop(0, tm, drain, 0, unroll=DMA_UNROLL)
    rt = route_ref[...]
    y = rt[:, 0:1] * _unpack_pairs(rows[0]) + rt[:, 1:2] * _unpack_pairs(rows[1])
    o_ref[...] = x_ref[...] + g2_ref[0] * y


def _combine(yb, dest_flat, route, x1, g2, seq, tm=256):
    T, D = x1.shape
    W = yb.shape[1]
    per = seq // tm
    B = T // seq
    return pl.pallas_call(
        _combine_kernel,
        grid=(T // tm,),
        in_specs=[pl.BlockSpec((TOP_K_INNER * tm,), lambda i: (i,), memory_space=pltpu.SMEM),
                  pl.BlockSpec(memory_space=pl.ANY),
                  pl.BlockSpec((tm, LANES), lambda i: (i, 0)),
                  pl.BlockSpec((tm, D), lambda i: (i, 0)),
                  pl.BlockSpec((1, 1, D), lambda i: (i // per, 0, 0))],
        out_specs=pl.BlockSpec((tm, D), lambda i: (i, 0)),
        out_shape=jax.ShapeDtypeStruct((T, D), F32),
        scratch_shapes=[pltpu.VMEM((TOP_K_INNER, tm, W), jnp.uint32), pltpu.SemaphoreType.DMA],
        compiler_params=_cparams(("arbitrary",)),
        name="moe_combine",
    )(dest_flat, yb, route, x1, g2.reshape(B, 1, D))


def _moe(h2p, route, x1, g2, w1, w3, w2, seq):
    T, D = x1.shape
    A = T * TOP_K_INNER
    n_blocks = -(-(A + N_EXPERTS * (MOE_BLOCK - 1)) // MOE_BLOCK)
    rank, cnt = _ranks(route)
    counts = cnt[0, :N_EXPERTS].astype(jnp.int32)
    padded = (counts + MOE_BLOCK - 1) // MOE_BLOCK * MOE_BLOCK
    pad_end = jnp.cumsum(padded)
    pad_start = pad_end - padded
    experts = route[:, 2:2 + TOP_K_INNER].astype(jnp.int32)
    dest = (pad_start[experts] + rank[:, :TOP_K_INNER].astype(jnp.int32)).reshape(A)
    blk_first = jnp.arange(n_blocks, dtype=jnp.int32) * MOE_BLOCK
    blk_e = jnp.minimum(jnp.sum((pad_end[None, :] <= blk_first[:, None]).astype(jnp.int32), axis=1),
                        N_EXPERTS - 1)
    n_used = (pad_end[-1:] // MOE_BLOCK).astype(jnp.int32)
    buf = _dispatch(h2p, dest, n_blocks * MOE_BLOCK)
    yb = _experts(buf, blk_e, n_used, w1, w3, w2)
    return _combine(yb, dest, route, x1, g2, seq)


def kernel(x, c, ada_w, ada_b, norm1_g, norm2_g, w_in, w_out, rwkv_mu, rwkv_w0, rwkv_w2, rwkv_a0, rwkv_a2, rwkv_g2, rwkv_kk, rwkv_ka, rwkv_rk, rwkv_ln_g, rwkv_ln_b, diff_q_gain, diff_k_gain, diff_lambda, diff_subln_g, moba_q_gain, moba_k_gain, rel_bias, router_g_w, router_g_b, router_e_w, router_e_b, moe_w1, moe_w3, moe_w2):
    B, S, D = x.shape
    T = B * S
    mods = _adaln(c, ada_w, ada_b)
    bias_df = _bias_tiles(rel_bias[:, :DF_HEADS])
    bias_mb = _bias_tiles(rel_bias[:, DF_HEADS:])
    n_route = N_GROUPS + N_EXPERTS
    for l in range(DEPTH):
        sh1, sc1, g1, sh2, sc2, g2 = [mods[l, :, j * D:(j + 1) * D] for j in range(6)]
        p_r, p_d, p_m = _ln_inproj(x, norm1_g[l], sc1, sh1, w_in[l].astype(BF16))
        r, k, v, kap, b, lw, g, bonus = _rwkv_prep(p_r, rwkv_mu[l], rwkv_w0[l], rwkv_w2[l], rwkv_a0[l],
                                                   rwkv_a2[l], rwkv_g2[l], rwkv_kk[l], rwkv_ka[l],
                                                   rwkv_rk[l])
        y_r = _rwkv_chunk(r, k, v, kap, b, lw, g, bonus, rwkv_ln_g[l], rwkv_ln_b[l])
        lambda_init = 0.8 - 0.6 * math.exp(-0.3 * l)
        y_d = _diff_attn(p_d, bias_df, diff_q_gain[l], diff_k_gain[l], diff_lambda[l],
                         diff_subln_g[l], lambda_init)
        y_m = _moba(p_m, bias_mb, moba_q_gain[l], moba_k_gain[l])
        w_router = jnp.zeros((D, LANES), F32).at[:, :n_route].set(
            jnp.concatenate([router_g_w[l], router_e_w[l]], axis=1))
        b_router = jnp.zeros((1, LANES), F32).at[0, :n_route].set(
            jnp.concatenate([router_g_b[l], router_e_b[l]]))
        x1, h2p, route = _outproj(y_r, y_d, y_m, x, g1, sc2, sh2, norm2_g[l], w_out[l].astype(BF16),
                                  w_router, b_router)
        x2 = _moe(h2p.reshape(T, D // 2), route.reshape(T, LANES), x1.reshape(T, D), g2,
                  moe_w1[l], moe_w3[l], moe_w2[l], S)
        x = x2.reshape(B, S, D)
    return x
```

```python
import functools
import math

import jax
import jax.numpy as jnp
from jax import lax
from jax.experimental import pallas as pl
from jax.experimental.pallas import tpu as pltpu

F32 = jnp.float32
BF16 = jnp.bfloat16
HIGHEST = lax.Precision.HIGHEST

D_MODEL = 1024
DEPTH = 4
HEAD_DIM = 64
R_HEADS = 4
R_WIDTH = 256
R_LORA_W = 32
R_LORA_A = 32
R_LORA_G = 64
R_COLS = 896
R_GN_EPS = 64e-5
DF_HEADS = 4
DF_V = 128
DF_WIDTH = 512
DF_COLS = 1536
MB_HEADS = 4
MB_WIDTH = 256
MB_COLS = 768
MB_BLOCK = 256
MB_TOPK = 3
IN_COLS = 3200
NUM_BUCKETS = 32
MAX_DISTANCE = 1024
N_GROUPS = 4
EXPERTS_PER_GROUP = 8
N_EXPERTS = 32
TOP_K_INNER = 2
EXPERT_HIDDEN = 512
MOE_BLOCK = 256
RMS_EPS = 1e-6

LANES = 128
ATT_TILE = 256
SAT_TILE = 5
MASK_TILE = 6
N_BIAS_TILES = 7
ATT_UNROLL = 2
Q_HALVES = 2
RW_CHUNK = 64
RW_CHUNKS_PER_STEP = 4
NEG_BIG = -1e30
LOG2E = math.log2(math.e)
QK_SCALE = HEAD_DIM ** -0.5 * LOG2E
VMEM_LIMIT = 48 * 1024 * 1024


def _cparams(sem):
    return pltpu.CompilerParams(dimension_semantics=sem, vmem_limit_bytes=VMEM_LIMIT)


def _mm(a, b):
    return jnp.dot(a.astype(BF16), b.astype(BF16), preferred_element_type=F32)


def _mm_nt(a, b):
    return lax.dot_general(a.astype(BF16), b.astype(BF16), (((1,), (1,)), ((), ())),
                           preferred_element_type=F32)


def _mm_tn(a, b):
    return lax.dot_general(a.astype(BF16), b.astype(BF16), (((0,), (0,)), ((), ())),
                           preferred_element_type=F32)


def _mm_hi(a, b):
    return jnp.dot(a, b, precision=HIGHEST, preferred_element_type=F32)


def _mm_3pass(a, b):
    a_hi = a.astype(BF16)
    a_lo = (a - a_hi.astype(F32)).astype(BF16)
    b_hi = b.astype(BF16)
    b_lo = (b - b_hi.astype(F32)).astype(BF16)
    m = a.shape[0]
    first = jnp.dot(jnp.concatenate([a_hi, a_lo], axis=0), b_hi, preferred_element_type=F32)
    return first[:m] + first[m:] + jnp.dot(a_hi, b_lo, preferred_element_type=F32)


def _adaln_kernel(c_ref, w_ref, b_ref, o_ref):
    c = c_ref[...]
    cs = c * jax.nn.sigmoid(c)
    o_ref[0] = _mm_hi(cs, w_ref[0]) + b_ref[0]


def _adaln(c, ada_w, ada_b):
    L, D, D6 = ada_w.shape
    B = c.shape[0]
    nj = D6 // D
    return pl.pallas_call(
        _adaln_kernel,
        grid=(L, nj),
        in_specs=[pl.BlockSpec((B, D), lambda l, j: (0, 0)),
                  pl.BlockSpec((1, D, D), lambda l, j: (l, 0, j)),
                  pl.BlockSpec((1, 1, D), lambda l, j: (l, 0, j))],
        out_specs=pl.BlockSpec((1, B, D), lambda l, j: (l, 0, j)),
        out_shape=jax.ShapeDtypeStruct((L, B, D6), F32),
        compiler_params=_cparams(("parallel", "parallel")),
        name="adaln",
    )(c, ada_w, ada_b.reshape(L, 1, D6))


def _ln_inproj_kernel(x_ref, g_ref, sc_ref, sh_ref, w_ref, pr_ref, pd_ref, pm_ref):
    x = x_ref[0]
    y = x * lax.rsqrt(jnp.mean(x * x, -1, keepdims=True) + RMS_EPS)
    h = ((y * g_ref[...]) * (1.0 + sc_ref[0]) + sh_ref[0]).astype(BF16)
    c1 = R_COLS
    c2 = R_COLS + DF_COLS
    pr_ref[0] = jnp.dot(h, w_ref[:, :c1], preferred_element_type=F32)
    pd_ref[0] = jnp.dot(h, w_ref[:, c1:c2], preferred_element_type=F32).astype(BF16)
    pm_ref[0] = jnp.dot(h, w_ref[:, c2:], preferred_element_type=F32).astype(BF16)


def _ln_inproj(x, g, sc, sh, w_bf16, tm=512):
    B, S, D = x.shape
    return pl.pallas_call(
        _ln_inproj_kernel,
        grid=(B, S // tm),
        in_specs=[pl.BlockSpec((1, tm, D), lambda b, i: (b, i, 0)),
                  pl.BlockSpec((1, D), lambda b, i: (0, 0)),
                  pl.BlockSpec((1, 1, D), lambda b, i: (b, 0, 0)),
                  pl.BlockSpec((1, 1, D), lambda b, i: (b, 0, 0)),
                  pl.BlockSpec((D, IN_COLS), lambda b, i: (0, 0))],
        out_specs=[pl.BlockSpec((1, tm, R_COLS), lambda b, i: (b, i, 0)),
                   pl.BlockSpec((1, tm, DF_COLS), lambda b, i: (b, i, 0)),
                   pl.BlockSpec((1, tm, MB_COLS), lambda b, i: (b, i, 0))],
        out_shape=[jax.ShapeDtypeStruct((B, S, R_COLS), F32),
                   jax.ShapeDtypeStruct((B, S, DF_COLS), BF16),
                   jax.ShapeDtypeStruct((B, S, MB_COLS), BF16)],
        compiler_params=_cparams(("parallel", "parallel")),
        name="ln_inproj",
    )(x, g.reshape(1, D), sc.reshape(B, 1, D), sh.reshape(B, 1, D), w_bf16)


def _head_ones(n):
    r = lax.broadcasted_iota(jnp.int32, (n, n), 0) // HEAD_DIM
    c = lax.broadcasted_iota(jnp.int32, (n, n), 1) // HEAD_DIM
    return (r == c).astype(F32)


def _rwkv_prep_kernel(p_ref, pp_ref, mu_ref, w0_ref, w2_ref, a0_ref, a2_ref, g2_ref, kk_ref, ka_ref,
                      rk_ref, r_o, k_o, v_o, kap_o, b_o, lw_o, g_o, bonus_o):
    i = pl.program_id(1)
    p = p_ref[0]
    prev_row = jnp.where(i > 0, pp_ref[0][7:8, :], 0.0)
    rows = lax.broadcasted_iota(jnp.int32, p.shape, 0)
    p_prev = jnp.where(rows == 0, prev_row, pltpu.roll(p, 1, 0))
    ps = p + (p_prev - p) * mu_ref[...]
    W = R_WIDTH
    r = ps[:, 0:W]
    k = ps[:, W:2 * W]
    v = ps[:, 2 * W:3 * W]
    c3 = 3 * W
    c4 = c3 + R_LORA_W
    c5 = c4 + R_LORA_A
    wd = ps[:, c3:c4]
    ad = ps[:, c4:c5]
    gd = ps[:, c5:R_COLS]
    z = -(w0_ref[...] + _mm_hi(jnp.tanh(wd), w2_ref[...]))
    softplus = jnp.maximum(z, 0.0) + jnp.log(1.0 + jnp.exp(-jnp.abs(z)))
    lw_o[0] = -jnp.exp(-softplus - 0.5)
    a = jax.nn.sigmoid(a0_ref[...] + _mm_hi(ad, a2_ref[...]))
    g_o[0] = _mm_hi(jax.nn.sigmoid(gd), g2_ref[...])
    ones = _head_ones(W)
    kk = k * kk_ref[...]
    nrm = jnp.sqrt(_mm_hi(kk * kk, ones))
    kap = kk / jnp.maximum(nrm, 1e-12)
    k2 = k * (1.0 + (a - 1.0) * ka_ref[...])
    r_o[0] = r
    k_o[0] = k2
    v_o[0] = v
    kap_o[0] = kap
    b_o[0] = kap * a
    bonus_o[0] = _mm_hi(r * k2 * rk_ref[...], ones) * v


def _rwkv_prep(p_r, mu, w0, w2, a0, a2, g2, k_k, k_a, r_k, tm=512):
    B, S, _ = p_r.shape
    W = R_WIDTH
    row = lambda a: a.reshape(1, -1)
    full = lambda a: pl.BlockSpec(a.shape, lambda b, i: (0,) * a.ndim)
    ins = [row(mu), row(w0), w2, row(a0), a2, g2, row(k_k), row(k_a), row(r_k)]
    out_spec = pl.BlockSpec((1, tm, W), lambda b, i: (b, i, 0))
    return pl.pallas_call(
        _rwkv_prep_kernel,
        grid=(B, S // tm),
        in_specs=[pl.BlockSpec((1, tm, R_COLS), lambda b, i: (b, i, 0)),
                  pl.BlockSpec((1, 8, R_COLS), lambda b, i: (b, jnp.maximum(i * (tm // 8) - 1, 0), 0))]
                 + [full(a) for a in ins],
        out_specs=[out_spec] * 8,
        out_shape=[jax.ShapeDtypeStruct((B, S, W), F32)] * 8,
        compiler_params=_cparams(("parallel", "arbitrary")),
        name="rwkv_prep",
    )(p_r, p_r, *ins)


def _rwkv_chunk_kernel(r_ref, k_ref, v_ref, kap_ref, b_ref, lw_ref, g_ref, bonus_ref, lng_ref, lnb_ref,
                       o_ref, state):
    step = pl.program_id(1)
    C = RW_CHUNK
    NC = RW_CHUNKS_PER_STEP
    Dh = HEAD_DIM
    TM = NC * C

    @pl.when(step == 0)
    def _():
        state[...] = jnp.zeros_like(state)

    ti = lax.broadcasted_iota(jnp.int32, (C, C), 0)
    tj = lax.broadcasted_iota(jnp.int32, (C, C), 1)
    eye = (ti == tj).astype(F32)
    ti2 = lax.broadcasted_iota(jnp.int32, (C, 2 * C), 0)
    tj2 = lax.broadcasted_iota(jnp.int32, (C, 2 * C), 1) % C
    strict2 = tj2 < ti2
    incl2 = tj2 <= ti2

    bi = lax.broadcasted_iota(jnp.int32, (TM, TM), 0)
    bj = lax.broadcasted_iota(jnp.int32, (TM, TM), 1)
    chunk_tri = jnp.logical_and(bi // C == bj // C, bj <= bi).astype(F32)

    lw = lw_ref[0]
    cum = _mm_hi(chunk_tri, lw)
    cum_end = jnp.concatenate(
        [jnp.broadcast_to(cum[(c + 1) * C - 1:(c + 1) * C, :], (C, R_WIDTH)) for c in range(NC)], axis=0)
    e_pos = jnp.exp(cum)
    e_neg = jnp.exp(-cum)
    e_prev = jnp.exp(cum - lw)
    e_end = jnp.exp(cum_end - cum)
    g_end = jnp.exp(cum_end)
    r = r_ref[0]
    k = k_ref[0]
    v = v_ref[0]
    b = b_ref[0]
    rt = r * e_pos
    kapt = kap_ref[0] * e_prev
    bt = b * e_neg
    kt = k * e_neg
    bh = b * e_end
    kh = k * e_end

    items = [(c, h) for c in range(NC) for h in range(R_HEADS)]

    def tile(x, c, h):
        return x[c * C:(c + 1) * C, h * Dh:(h + 1) * Dh]

    a_top, a_bot = [], []
    for c, h in items:
        lhs = jnp.concatenate([tile(kapt, c, h), tile(rt, c, h)], axis=0)
        rhs = jnp.concatenate([tile(bt, c, h), tile(kt, c, h)], axis=0)
        a = _mm_nt(lhs, rhs)
        a_top.append(jnp.where(strict2, a[:C], 0.0))
        a_bot.append(jnp.where(incl2, a[C:], 0.0))
    xp = [-t[:, :C] for t in a_top]
    tinv = [eye + x for x in xp]
    xp = [_mm(x, x) for x in xp]
    for _ in range(4):
        prod = [_mm(jnp.concatenate([t, x], axis=0), x) for t, x in zip(tinv, xp)]
        tinv = [t + p[:C] for t, p in zip(tinv, prod)]
        xp = [p[C:] for p in prod]
    tinv = [t + _mm(t, x) for t, x in zip(tinv, xp)]
    vt = [tile(v, c, h) for c, h in items]
    lkv = [_mm(t[:, C:], vv) for t, vv in zip(a_top, vt)]
    w = [_mm(t, tile(kapt, c, h)) for t, (c, h) in zip(tinv, items)]
    ploc = [-_mm(t, x) for t, x in zip(tinv, lkv)]
    q_eff = [tile(rt, c, h) - _mm(ab[:, :C], ww) for ab, ww, (c, h) in zip(a_bot, w, items)]
    pv = [jnp.concatenate([p, vv], axis=0) for p, vv in zip(ploc, vt)]
    y_loc = [_mm(ab, x) for ab, x in zip(a_bot, pv)]
    g_mat = [eye * tile(g_end, c, h)[0:1, :] - _mm_tn(ww, tile(bh, c, h)) for ww, (c, h) in zip(w, items)]
    u_mat = [_mm_tn(x, jnp.concatenate([tile(bh, c, h), tile(kh, c, h)], axis=0))
             for x, (c, h) in zip(pv, items)]

    s_cur = [state[h] for h in range(R_HEADS)]
    ys = [[None] * NC for _ in range(R_HEADS)]
    for n, (c, h) in enumerate(items):
        ys[h][c] = _mm_nt(q_eff[n], s_cur[h]) + y_loc[n]
        s_cur[h] = _mm(s_cur[h], g_mat[n]) + u_mat[n]
    for h in range(R_HEADS):
        state[h] = s_cur[h]

    gate = g_ref[0]
    bonus = bonus_ref[0]
    lng = lng_ref[...]
    lnb = lnb_ref[...]
    lane = lax.broadcasted_iota(jnp.int32, (Dh, R_WIDTH), 1)
    sub = lax.broadcasted_iota(jnp.int32, (Dh, R_WIDTH), 0)
    out = jnp.zeros((TM, R_WIDTH), F32)
    for h in range(R_HEADS):
        sl = slice(h * Dh, (h + 1) * Dh)
        y = jnp.concatenate(ys[h], axis=0)
        mean = jnp.mean(y, -1, keepdims=True)
        yc = y - mean
        var = jnp.mean(yc * yc, -1, keepdims=True)
        yn = yc * lax.rsqrt(var + R_GN_EPS)
        yfin = (yn * lng[:, sl] + lnb[:, sl] + bonus[:, sl]) * gate[:, sl]
        place = (lane == sub + h * Dh).astype(BF16)
        out = out + jnp.dot(yfin.astype(BF16), place, preferred_element_type=F32)
    o_ref[0] = out.astype(BF16)


def _rwkv_chunk(r, k, v, kap, b, lw, g, bonus, ln_g, ln_b):
    B, S, W = r.shape
    tm = RW_CHUNK * RW_CHUNKS_PER_STEP
    spec = pl.BlockSpec((1, tm, W), lambda bb, c: (bb, c, 0))
    vec = pl.BlockSpec((1, W), lambda bb, c: (0, 0))
    return pl.pallas_call(
        _rwkv_chunk_kernel,
        grid=(B, S // tm),
        in_specs=[spec] * 8 + [vec, vec],
        out_specs=spec,
        out_shape=jax.ShapeDtypeStruct((B, S, W), BF16),
        scratch_shapes=[pltpu.VMEM((R_HEADS, HEAD_DIM, HEAD_DIM), F32)],
        compiler_params=_cparams(("parallel", "arbitrary")),
        name="rwkv_chunk",
    )(r, k, v, kap, b, lw, g, bonus, ln_g.reshape(1, W), ln_b.reshape(1, W))


def _t5_bucket(dist):
    n = jnp.maximum(dist, 0)
    max_exact = NUM_BUCKETS // 2
    nf = jnp.maximum(n, 1).astype(F32)
    large = max_exact + (jnp.log(nf / max_exact) / math.log(MAX_DISTANCE / max_exact)
                         * (NUM_BUCKETS - max_exact)).astype(jnp.int32)
    large = jnp.minimum(large, NUM_BUCKETS - 1)
    return jnp.where(n < max_exact, n, large)


def _bias_tiles(tbl):
    T = ATT_TILE
    assert (SAT_TILE - 1) * T + 1 >= MAX_DISTANCE
    nh = tbl.shape[1]
    o = jnp.arange(SAT_TILE + 1)[:, None, None]
    i = jnp.arange(T)[None, :, None]
    j = jnp.arange(T)[None, None, :]
    dist = o * T + i - j
    onehot = (_t5_bucket(dist)[None] == jnp.arange(NUM_BUCKETS)[:, None, None, None]).astype(F32)
    tiles = jnp.dot(tbl.astype(F32).T, onehot.reshape(NUM_BUCKETS, -1), precision=HIGHEST)
    tiles = jnp.where((dist >= 0)[None], tiles.reshape(nh, SAT_TILE + 1, T, T) * LOG2E, NEG_BIG)
    return jnp.concatenate([tiles, jnp.full((nh, 1, T, T), NEG_BIG, F32)], axis=1)


def _half_masks(shape):
    lane = lax.broadcasted_iota(jnp.int32, shape, len(shape) - 1)
    lo = lane < HEAD_DIM
    return lo, jnp.logical_not(lo)


def _rms_halves(x, gain2):
    lo, hi = _half_masks(x.shape)
    sq = x * x
    s_lo = jnp.sum(jnp.where(lo, sq, 0.0), -1, keepdims=True)
    s_hi = jnp.sum(jnp.where(hi, sq, 0.0), -1, keepdims=True)
    inv = jnp.where(lo, lax.rsqrt(s_lo / HEAD_DIM + RMS_EPS), lax.rsqrt(s_hi / HEAD_DIM + RMS_EPS))
    return x * inv * gain2


def _lane_chunks(x):
    return [x[:, c * LANES:(c + 1) * LANES] for c in range(x.shape[1] // LANES)]


def _flash(q2, kn, v_ref, bias_tile, qi, s_bufs, m_s, l_s, acc_s):
    T = ATT_TILE
    U = ATT_UNROLL
    H = Q_HALVES
    n_iter = (qi * H + H - 1) // U + 1

    def window(i):
        return pl.ds(pl.multiple_of(jnp.minimum(i, n_iter - 1) * (U * T), U * T), U * T)

    def produce(i, dst):
        s = lax.dot_general(q2, kn[window(i), :], (((1,), (1,)), ((), ())), preferred_element_type=F32)
        idx = {}
        for d in range(-(U - 1), H):
            off = qi * H - i * U + d
            idx[d] = jnp.where(off < 0, MASK_TILE, jnp.minimum(off, SAT_TILE))
        bias = jnp.concatenate(
            [jnp.concatenate([bias_tile(g, idx[a - u]) for u in range(U)], axis=1)
             for g in range(2) for a in range(H)], axis=0)
        dst[...] = s + bias

    def consume(i, src):
        parts = _lane_chunks(src[...])
        m_old = m_s[...]
        blk_max = jnp.max(functools.reduce(jnp.maximum, parts), -1, keepdims=True)
        m_new = jnp.maximum(m_old, jnp.broadcast_to(blk_max, m_old.shape))
        alpha = jnp.exp2(m_old - m_new)
        ps = [jnp.exp2(part - m_new) for part in parts]
        l_s[...] = alpha * l_s[...] + functools.reduce(lambda a, b: a + b, ps)
        p = jnp.concatenate(ps, axis=1).astype(BF16)
        acc_s[...] = alpha * acc_s[...] + jnp.dot(p, v_ref[0, window(i), :], preferred_element_type=F32)
        m_s[...] = m_new

    m_s[...] = jnp.full_like(m_s, -jnp.inf)
    l_s[...] = jnp.zeros_like(l_s)
    acc_s[...] = jnp.zeros_like(acc_s)
    produce(0, s_bufs[0])

    def body(i, carry):
        for par in range(2):
            @pl.when(i % 2 == par)
            def _():
                produce(i + 1, s_bufs[1 - par])
                consume(i, s_bufs[par])
        return carry

    lax.fori_loop(0, n_iter, body, 0)
    return acc_s[...] / jnp.sum(l_s[...], -1, keepdims=True)


def _diff_attn_kernel(q_ref, k_ref, v_ref, bias_ref, qg_ref, kg_ref, lam_ref, sg_ref, o_ref,
                      kn, s_a, s_b, m_s, l_s, acc_s, *, lambda_init):
    qi = pl.program_id(2)
    T = ATT_TILE
    nkb = k_ref.shape[1] // T

    @pl.when(qi == 0)
    def _():
        def body(j, carry):
            rows = pl.ds(pl.multiple_of(j * T, T), T)
            kn[rows, :] = _rms_halves(k_ref[0, rows, :].astype(F32), kg_ref[...]).astype(BF16)
            return carry
        lax.fori_loop(0, nkb, body, 0)

    q = _rms_halves(q_ref[0].astype(F32), qg_ref[...]) * QK_SCALE
    tq = q.shape[0]
    lo, hi = _half_masks(q.shape)
    q2 = jnp.concatenate([jnp.where(lo, q, 0.0), jnp.where(hi, q, 0.0)], axis=0).astype(BF16)

    def bias_tile(g, idx):
        return bias_ref[0, idx]

    o = _flash(q2, kn, v_ref, bias_tile, qi, (s_a, s_b), m_s, l_s, acc_s)
    lam = lam_ref[...]
    lam_full = (jnp.exp(jnp.sum(lam[0:1] * lam[1:2], -1, keepdims=True))
                - jnp.exp(jnp.sum(lam[2:3] * lam[3:4], -1, keepdims=True)) + lambda_init)
    out = o[:tq] - lam_full * o[tq:]
    out = out * lax.rsqrt(jnp.mean(out * out, -1, keepdims=True) + RMS_EPS) * sg_ref[...]
    o_ref[0] = (out * (1.0 - lambda_init)).astype(BF16)


def _diff_attn(p_d, bias_tiles, q_gain, k_gain, lam, subln_g, lambda_init):
    B, S, _ = p_d.shape
    T = ATT_TILE
    H = DF_HEADS
    TQ = Q_HALVES * T
    R = 2 * TQ
    assert S % (ATT_UNROLL * T) == 0 and S % TQ == 0
    gain2 = lambda g: jnp.concatenate([g, g]).reshape(1, 2 * HEAD_DIM)
    kern = functools.partial(_diff_attn_kernel, lambda_init=lambda_init)
    return pl.pallas_call(
        kern,
        grid=(B, H, S // TQ),
        in_specs=[pl.BlockSpec((1, TQ, LANES), lambda b, h, i: (b, i, h)),
                  pl.BlockSpec((1, S, LANES), lambda b, h, i: (b, 0, H + h)),
                  pl.BlockSpec((1, S, LANES), lambda b, h, i: (b, 0, 2 * H + h)),
                  pl.BlockSpec((1, N_BIAS_TILES, T, T), lambda b, h, i: (h, 0, 0, 0)),
                  pl.BlockSpec((1, LANES), lambda b, h, i: (0, 0)),
                  pl.BlockSpec((1, LANES), lambda b, h, i: (0, 0)),
                  pl.BlockSpec((4, HEAD_DIM), lambda b, h, i: (0, 0)),
                  pl.BlockSpec((1, DF_V), lambda b, h, i: (0, 0))],
        out_specs=pl.BlockSpec((1, TQ, DF_V), lambda b, h, i: (b, i, h)),
        out_shape=jax.ShapeDtypeStruct((B, S, DF_WIDTH), BF16),
        scratch_shapes=[pltpu.VMEM((S, LANES), BF16),
                        pltpu.VMEM((R, ATT_UNROLL * T), F32),
                        pltpu.VMEM((R, ATT_UNROLL * T), F32),
                        pltpu.VMEM((R, LANES), F32),
                        pltpu.VMEM((R, LANES), F32),
                        pltpu.VMEM((R, DF_V), F32)],
        compiler_params=_cparams(("parallel", "parallel", "arbitrary")),
        name="diff_attn",
    )(p_d, p_d, p_d, bias_tiles, gain2(q_gain), gain2(k_gain), lam, subln_g.reshape(1, DF_V))


def _moba_kernel(q_ref, k_ref, v_ref, bias_ref, qg_ref, kg_ref, o_ref, kn, kmean, s_a, s_b, m_s, l_s,
                 acc_s):
    qi = pl.program_id(2)
    T = ATT_TILE
    nkb = k_ref.shape[1] // T

    @pl.when(qi == 0)
    def _():
        kmean[...] = jnp.zeros_like(kmean)
        lane = lax.broadcasted_iota(jnp.int32, (T, LANES), 1)

        def body(j, carry):
            rows = pl.ds(pl.multiple_of(j * T, T), T)
            kf = _rms_halves(k_ref[0, rows, :].astype(F32), kg_ref[...])
            kn[rows, :] = jnp.concatenate([kf.astype(BF16), (lane == j).astype(BF16)], axis=1)
            kmean[pl.ds(j, 1), :] = jnp.mean(kf, 0, keepdims=True)
            return carry
        lax.fori_loop(0, nkb, body, 0)

    q = _rms_halves(q_ref[0].astype(F32), qg_ref[...])
    tq = q.shape[0]
    lo, hi = _half_masks(q.shape)
    q2f = jnp.concatenate([jnp.where(lo, q, 0.0), jnp.where(hi, q, 0.0)], axis=0)

    nkp = kmean.shape[0]
    km = kmean[...]
    km_hi = km.astype(BF16)
    km_lo = (km - km_hi.astype(F32)).astype(BF16)
    q_hi = q2f.astype(BF16)
    q_lo = (q2f - q_hi.astype(F32)).astype(BF16)
    nt = (((1,), (1,)), ((), ()))
    g_hi = lax.dot_general(jnp.concatenate([km_hi, km_lo], axis=0), q_hi, nt, preferred_element_type=F32)
    gate = g_hi[:nkp] + g_hi[nkp:] + lax.dot_general(km_hi, q_lo, nt, preferred_element_type=F32)
    blk = lax.broadcasted_iota(jnp.int32, gate.shape, 0).astype(F32)
    col = lax.broadcasted_iota(jnp.int32, gate.shape, 1)
    own = (qi * Q_HALVES + (col % tq) // T).astype(F32)
    past = blk < own
    gate = jnp.where(past, gate, -jnp.inf)
    pen_t = jnp.where(past, NEG_BIG, 0.0)
    for _ in range(MB_TOPK):
        mx = jnp.max(gate, 0, keepdims=True)
        first = jnp.min(jnp.where(gate == mx, blk, float(nkp)), 0, keepdims=True)
        pick = jnp.logical_and(blk == first, mx > -jnp.inf)
        pen_t = jnp.where(pick, 0.0, pen_t)
        gate = jnp.where(pick, -jnp.inf, gate)
    pen = jnp.concatenate([pen_t, jnp.zeros((LANES - nkp, 2 * tq), F32)], axis=0).T
    q2 = jnp.concatenate([(q2f * QK_SCALE).astype(BF16), pen.astype(BF16)], axis=1)

    def bias_tile(g, idx):
        return bias_ref[g, idx]

    o = _flash(q2, kn, v_ref, bias_tile, qi, (s_a, s_b), m_s, l_s, acc_s)
    lo_o, _ = _half_masks((tq, LANES))
    o_ref[0] = jnp.where(lo_o, o[:tq], o[tq:]).astype(BF16)


def _moba(p_m, bias_tiles, q_gain, k_gain):
    B, S, _ = p_m.shape
    T = ATT_TILE
    HP = MB_HEADS // 2
    TQ = Q_HALVES * T
    R = 2 * TQ
    assert T == MB_BLOCK and S // T <= LANES and S % (ATT_UNROLL * T) == 0 and S % TQ == 0
    gain2 = lambda g: jnp.concatenate([g, g]).reshape(1, 2 * HEAD_DIM)
    return pl.pallas_call(
        _moba_kernel,
        grid=(B, HP, S // TQ),
        in_specs=[pl.BlockSpec((1, TQ, LANES), lambda b, h, i: (b, i, h)),
                  pl.BlockSpec((1, S, LANES), lambda b, h, i: (b, 0, HP + h)),
                  pl.BlockSpec((1, S, LANES), lambda b, h, i: (b, 0, 2 * HP + h)),
                  pl.BlockSpec((2, N_BIAS_TILES, T, T), lambda b, h, i: (h, 0, 0, 0)),
                  pl.BlockSpec((1, LANES), lambda b, h, i: (0, 0)),
                  pl.BlockSpec((1, LANES), lambda b, h, i: (0, 0))],
        out_specs=pl.BlockSpec((1, TQ, LANES), lambda b, h, i: (b, i, h)),
        out_shape=jax.ShapeDtypeStruct((B, S, MB_WIDTH), BF16),
        scratch_shapes=[pltpu.VMEM((S, 2 * LANES), BF16),
                        pltpu.VMEM((-(-(S // T) // 8) * 8, LANES), F32),
                        pltpu.VMEM((R, ATT_UNROLL * T), F32),
                        pltpu.VMEM((R, ATT_UNROLL * T), F32),
                        pltpu.VMEM((R, LANES), F32),
                        pltpu.VMEM((R, LANES), F32),
                        pltpu.VMEM((R, LANES), F32)],
        compiler_params=_cparams(("parallel", "parallel", "arbitrary")),
        name="moba",
    )(p_m, p_m, p_m, bias_tiles, gain2(q_gain), gain2(k_gain))


def _pack_pairs(x):
    n = x.shape[1] // 2
    hi = pltpu.bitcast(x[:, :n].astype(BF16).astype(F32), jnp.uint32)
    lo = pltpu.bitcast(x[:, n:].astype(BF16).astype(F32), jnp.uint32)
    return hi | (lo >> 16)


def _unpack_pairs(p):
    hi = pltpu.bitcast(p & jnp.uint32(0xFFFF0000), F32)
    lo = pltpu.bitcast(p << 16, F32)
    return jnp.concatenate([hi, lo], axis=1)


def _outproj_kernel(yr_ref, yd_ref, ym_ref, x_ref, g1_ref, sc_ref, sh_ref, ng_ref, w_ref, wr_ref, br_ref,
                    x1_ref, h2_ref, route_ref):
    c1 = R_WIDTH
    c2 = R_WIDTH + DF_WIDTH
    mix = (jnp.dot(yr_ref[0], w_ref[:c1, :], preferred_element_type=F32)
           + jnp.dot(yd_ref[0], w_ref[c1:c2, :], preferred_element_type=F32)
           + jnp.dot(ym_ref[0], w_ref[c2:, :], preferred_element_type=F32))
    x1 = x_ref[0] + g1_ref[0] * mix
    x1_ref[0] = x1
    y = x1 * lax.rsqrt(jnp.mean(x1 * x1, -1, keepdims=True) + RMS_EPS)
    h2 = (y * ng_ref[...]) * (1.0 + sc_ref[0]) + sh_ref[0]
    h2_ref[0] = _pack_pairs(h2)

    logits = _mm_3pass(h2, wr_ref[...]) + br_ref[...]
    lane = lax.broadcasted_iota(jnp.int32, logits.shape, 1).astype(F32)
    far = float(LANES)
    is_g = lane < N_GROUPS
    gl = jnp.where(is_g, logits, -jnp.inf)
    gmax = jnp.max(gl, -1, keepdims=True)
    g_idx = jnp.min(jnp.where(gl == gmax, lane, far), -1, keepdims=True)
    pg_top = 1.0 / jnp.sum(jnp.where(is_g, jnp.exp(logits - gmax), 0.0), -1, keepdims=True)
    e_lo = N_GROUPS + EXPERTS_PER_GROUP * g_idx
    in_grp = jnp.logical_and(lane >= e_lo, lane < e_lo + EXPERTS_PER_GROUP)
    el = jnp.where(in_grp, logits, -jnp.inf)
    m1 = jnp.max(el, -1, keepdims=True)
    i1 = jnp.min(jnp.where(el == m1, lane, far), -1, keepdims=True)
    el2 = jnp.where(lane == i1, -jnp.inf, el)
    m2 = jnp.max(el2, -1, keepdims=True)
    i2 = jnp.min(jnp.where(el2 == m2, lane, far), -1, keepdims=True)
    e2 = jnp.exp(m2 - m1)
    gate1 = pg_top / (1.0 + e2)
    gate2 = gate1 * e2
    route = jnp.where(lane == 0, gate1, 0.0)
    route = jnp.where(lane == 1, gate2, route)
    route = jnp.where(lane == 2, i1 - N_GROUPS, route)
    route = jnp.where(lane == 3, i2 - N_GROUPS, route)
    route_ref[0] = route


def _outproj(y_r, y_d, y_m, x, g1, sc2, sh2, norm2_g, w_out_bf16, w_router, b_router, tm=512):
    B, S, D = x.shape
    vec = pl.BlockSpec((1, 1, D), lambda b, i: (b, 0, 0))
    tok = lambda w: pl.BlockSpec((1, tm, w), lambda b, i: (b, i, 0))
    full = lambda a: pl.BlockSpec(a.shape, lambda b, i: (0,) * a.ndim)
    return pl.pallas_call(
        _outproj_kernel,
        grid=(B, S // tm),
        in_specs=[tok(R_WIDTH), tok(DF_WIDTH), tok(MB_WIDTH), tok(D), vec, vec, vec,
                  pl.BlockSpec((1, D), lambda b, i: (0, 0)), full(w_out_bf16), full(w_router), full(b_router)],
        out_specs=[tok(D), tok(D // 2), tok(LANES)],
        out_shape=[jax.ShapeDtypeStruct((B, S, D), F32),
                   jax.ShapeDtypeStruct((B, S, D // 2), jnp.uint32),
                   jax.ShapeDtypeStruct((B, S, LANES), F32)],
        compiler_params=_cparams(("parallel", "parallel")),
        name="outproj_router",
    )(y_r, y_d, y_m, x, g1.reshape(B, 1, D), sc2.reshape(B, 1, D), sh2.reshape(B, 1, D),
      norm2_g.reshape(1, D), w_out_bf16, w_router, b_router)


def _rank_kernel(route_ref, rank_ref, cnt_ref, carry):
    i = pl.program_id(0)

    @pl.when(i == 0)
    def _():
        carry[...] = jnp.zeros_like(carry)

    rt = route_ref[...]
    tm = rt.shape[0]
    lane = lax.broadcasted_iota(jnp.int32, rt.shape, 1).astype(F32)
    oh0 = (lane == rt[:, 2:3]).astype(F32)
    oh1 = (lane == rt[:, 3:4]).astype(F32)
    oh = oh0 + oh1
    row = lax.broadcasted_iota(jnp.int32, (tm, tm), 0)
    col = lax.broadcasted_iota(jnp.int32, (tm, tm), 1)
    before = jnp.dot((col < row).astype(BF16), oh.astype(BF16), preferred_element_type=F32) + carry[...]
    r0 = jnp.sum(oh0 * before, -1, keepdims=True)
    r1 = jnp.sum(oh1 * before, -1, keepdims=True)
    rank_ref[...] = jnp.where(lane == 0, r0, jnp.where(lane == 1, r1, 0.0))
    carry[...] = carry[...] + jnp.sum(oh, 0, keepdims=True)
    cnt_ref[...] = carry[...]


def _ranks(route, tm=512):
    T = route.shape[0]
    return pl.pallas_call(
        _rank_kernel,
        grid=(T // tm,),
        in_specs=[pl.BlockSpec((tm, LANES), lambda i: (i, 0))],
        out_specs=[pl.BlockSpec((tm, LANES), lambda i: (i, 0)), pl.BlockSpec((1, LANES), lambda i: (0, 0))],
        out_shape=[jax.ShapeDtypeStruct((T, LANES), F32), jax.ShapeDtypeStruct((1, LANES), F32)],
        scratch_shapes=[pltpu.VMEM((1, LANES), F32)],
        compiler_params=_cparams(("arbitrary",)),
        name="moe_ranks",
    )(route)


DMA_UNROLL = 8


def _dispatch_kernel(dest_ref, h_ref, buf_in, buf_out, sem):
    del buf_in
    tm = h_ref.shape[0]

    def row_copy(t, slot):
        return pltpu.make_async_copy(h_ref.at[pl.ds(t, 1)], buf_out.at[pl.ds(slot, 1)], sem)

    def issue(t, carry):
        for k in range(TOP_K_INNER):
            row_copy(t, dest_ref[TOP_K_INNER * t + k]).start(priority=k % 2)
        return carry

    def drain(t, carry):
        for k in range(TOP_K_INNER):
            row_copy(0, 0).wait()
        return carry

    lax.fori_loop(0, tm, issue, 0, unroll=DMA_UNROLL)
    lax.fori_loop(0, tm, drain, 0, unroll=DMA_UNROLL)


def _dispatch(h2p, dest_flat, n_slots, tm=512):
    T, W = h2p.shape
    buf0 = jnp.zeros((n_slots, W), jnp.uint32)
    return pl.pallas_call(
        _dispatch_kernel,
        grid=(T // tm,),
        in_specs=[pl.BlockSpec((TOP_K_INNER * tm,), lambda i: (i,), memory_space=pltpu.SMEM),
                  pl.BlockSpec((tm, W), lambda i: (i, 0)),
                  pl.BlockSpec(memory_space=pl.ANY)],
        out_specs=pl.BlockSpec(memory_space=pl.ANY),
        out_shape=jax.ShapeDtypeStruct((n_slots, W), jnp.uint32),
        scratch_shapes=[pltpu.SemaphoreType.DMA],
        input_output_aliases={2: 0},
        compiler_params=_cparams(("arbitrary",)),
        name="moe_dispatch",
    )(dest_flat, h2p, buf0)


def _expert_kernel(be_ref, slot_ref, nxt_ref, nu_ref, x_ref, w1_hbm, w3_hbm, w2_hbm, o_ref,
                   w1_s, w3_s, w2_s, sems, *, layer):
    i = pl.program_id(0)
    used = i < nu_ref[0]
    e = be_ref[i]
    slot = slot_ref[i]
    run_start = jnp.logical_or(i == 0, e != be_ref[jnp.maximum(i - 1, 0)])

    def fetch(expert, s):
        return [pltpu.make_async_copy(w1_hbm.at[layer, expert], w1_s.at[s], sems.at[s, 0]),
                pltpu.make_async_copy(w3_hbm.at[layer, expert], w3_s.at[s], sems.at[s, 1]),
                pltpu.make_async_copy(w2_hbm.at[layer, expert], w2_s.at[s], sems.at[s, 2])]

    @pl.when(jnp.logical_and(used, run_start))
    def _():
        @pl.when(i == 0)
        def _():
            for cp in fetch(e, slot):
                cp.start()
        for cp in fetch(e, slot):
            cp.wait()
        nxt = nxt_ref[i]

        @pl.when(nxt >= 0)
        def _():
            for cp in fetch(nxt, 1 - slot):
                cp.start()

    @pl.when(used)
    def _():
        x = _unpack_pairs(x_ref[...]).astype(BF16)
        h1 = jnp.dot(x, w1_s[slot].astype(BF16), preferred_element_type=F32)
        h3 = jnp.dot(x, w3_s[slot].astype(BF16), preferred_element_type=F32)
        h = (h1 * jax.nn.sigmoid(h1) * h3).astype(BF16)
        o_ref[...] = _pack_pairs(jnp.dot(h, w2_s[slot].astype(BF16), preferred_element_type=F32))

    @pl.when(jnp.logical_not(used))
    def _():
        o_ref[...] = jnp.zeros_like(o_ref)


def _experts(buf, blk_e, counts, n_used, w1, w3, w2, layer):
    n_slots, W = buf.shape
    _, E, D, Hd = w1.shape
    nb = n_slots // MOE_BLOCK
    change = jnp.concatenate([jnp.ones((1,), jnp.int32), (blk_e[1:] != blk_e[:-1]).astype(jnp.int32)])
    slot = (jnp.cumsum(change) - 1) % 2
    ids = jnp.arange(E, dtype=jnp.int32)
    later = jnp.logical_and(ids[None, :] > ids[:, None], counts[None, :] > 0)
    nxt_e = jnp.min(jnp.where(later, ids[None, :], E), axis=1)
    nxt_e = jnp.where(nxt_e == E, -1, nxt_e)
    nxt = jnp.sum(jnp.where(blk_e[:, None] == ids[None, :], nxt_e[None, :], 0), axis=1)
    any_spec = pl.BlockSpec(memory_space=pl.ANY)
    grid_spec = pltpu.PrefetchScalarGridSpec(
        num_scalar_prefetch=4,
        grid=(nb,),
        in_specs=[pl.BlockSpec((MOE_BLOCK, W), lambda i, *_: (i, 0)), any_spec, any_spec, any_spec],
        out_specs=pl.BlockSpec((MOE_BLOCK, W), lambda i, *_: (i, 0)),
        scratch_shapes=[pltpu.VMEM((2, D, Hd), F32), pltpu.VMEM((2, D, Hd), F32),
                        pltpu.VMEM((2, Hd, D), F32), pltpu.SemaphoreType.DMA((2, 3))],
    )
    return pl.pallas_call(
        functools.partial(_expert_kernel, layer=layer),
        grid_spec=grid_spec,
        out_shape=jax.ShapeDtypeStruct((n_slots, W), jnp.uint32),
        compiler_params=_cparams(("arbitrary",)),
        name="moe_experts",
    )(blk_e, slot.astype(jnp.int32), nxt.astype(jnp.int32), n_used, buf, w1, w3, w2)


def _combine_kernel(dest_ref, yb_ref, route_ref, x_ref, g2_ref, o_ref, rows, sem):
    tm = x_ref.shape[0]

    def row_copy(t, k, slot):
        return pltpu.make_async_copy(yb_ref.at[pl.ds(slot, 1)], rows.at[k, pl.ds(t, 1)], sem)

    def issue(t, carry):
        for k in range(TOP_K_INNER):
            row_copy(t, k, dest_ref[TOP_K_INNER * t + k]).start(priority=k % 2)
        return carry

    def drain(t, carry):
        for k in range(TOP_K_INNER):
            row_copy(0, k, 0).wait()
        return carry

    lax.fori_loop(0, tm, issue, 0, unroll=DMA_UNROLL)
    lax.fori_loop(0, tm, drain, 0, unroll=DMA_UNROLL)
    rt = route_ref[...]
    y = rt[:, 0:1] * _unpack_pairs(rows[0]) + rt[:, 1:2] * _unpack_pairs(rows[1])
    o_ref[...] = x_ref[...] + g2_ref[0] * y


def _combine(yb, dest_flat, route, x1, g2, seq, tm=256):
    T, D = x1.shape
    W = yb.shape[1]
    per = seq // tm
    B = T // seq
    return pl.pallas_call(
        _combine_kernel,
        grid=(T // tm,),
        in_specs=[pl.BlockSpec((TOP_K_INNER * tm,), lambda i: (i,), memory_space=pltpu.SMEM),
                  pl.BlockSpec(memory_space=pl.ANY),
                  pl.BlockSpec((tm, LANES), lambda i: (i, 0)),
                  pl.BlockSpec((tm, D), lambda i: (i, 0)),
                  pl.BlockSpec((1, 1, D), lambda i: (i // per, 0, 0))],
        out_specs=pl.BlockSpec((tm, D), lambda i: (i, 0)),
        out_shape=jax.ShapeDtypeStruct((T, D), F32),
        scratch_shapes=[pltpu.VMEM((TOP_K_INNER, tm, W), jnp.uint32), pltpu.SemaphoreType.DMA],
        compiler_params=_cparams(("arbitrary",)),
        name="moe_combine",
    )(dest_flat, yb, route, x1, g2.reshape(B, 1, D))


def _moe(h2p, route, x1, g2, w1, w3, w2, seq, layer):
    T, D = x1.shape
    A = T * TOP_K_INNER
    n_blocks = -(-(A + N_EXPERTS * (MOE_BLOCK - 1)) // MOE_BLOCK)
    rank, cnt = _ranks(route)
    counts = cnt[0, :N_EXPERTS].astype(jnp.int32)
    padded = (counts + MOE_BLOCK - 1) // MOE_BLOCK * MOE_BLOCK
    pad_end = jnp.cumsum(padded)
    pad_start = pad_end - padded
    experts = route[:, 2:2 + TOP_K_INNER].astype(jnp.int32)
    is_e = experts[:, :, None] == jnp.arange(N_EXPERTS, dtype=jnp.int32)
    dest = (jnp.sum(jnp.where(is_e, pad_start, 0), axis=-1) + rank[:, :TOP_K_INNER].astype(jnp.int32)).reshape(A)
    blk_first = jnp.arange(n_blocks, dtype=jnp.int32) * MOE_BLOCK
    blk_e = jnp.minimum(jnp.sum((pad_end[None, :] <= blk_first[:, None]).astype(jnp.int32), axis=1),
                        N_EXPERTS - 1)
    n_used = (pad_end[-1:] // MOE_BLOCK).astype(jnp.int32)
    buf = _dispatch(h2p, dest, n_blocks * MOE_BLOCK)
    yb = _experts(buf, blk_e, counts, n_used, w1, w3, w2, layer)
    return _combine(yb, dest, route, x1, g2, seq)


def kernel(x, c, ada_w, ada_b, norm1_g, norm2_g, w_in, w_out, rwkv_mu, rwkv_w0, rwkv_w2, rwkv_a0, rwkv_a2, rwkv_g2, rwkv_kk, rwkv_ka, rwkv_rk, rwkv_ln_g, rwkv_ln_b, diff_q_gain, diff_k_gain, diff_lambda, diff_subln_g, moba_q_gain, moba_k_gain, rel_bias, router_g_w, router_g_b, router_e_w, router_e_b, moe_w1, moe_w3, moe_w2):
    B, S, D = x.shape
    T = B * S
    mods = _adaln(c, ada_w, ada_b)
    bias_df = _bias_tiles(rel_bias[:, :DF_HEADS])
    bias_mb = _bias_tiles(rel_bias[:, DF_HEADS:])
    n_route = N_GROUPS + N_EXPERTS
    for l in range(DEPTH):
        sh1, sc1, g1, sh2, sc2, g2 = [mods[l, :, j * D:(j + 1) * D] for j in range(6)]
        p_r, p_d, p_m = _ln_inproj(x, norm1_g[l], sc1, sh1, w_in[l].astype(BF16))
        r, k, v, kap, b, lw, g, bonus = _rwkv_prep(p_r, rwkv_mu[l], rwkv_w0[l], rwkv_w2[l], rwkv_a0[l],
                                                   rwkv_a2[l], rwkv_g2[l], rwkv_kk[l], rwkv_ka[l],
                                                   rwkv_rk[l])
        y_r = _rwkv_chunk(r, k, v, kap, b, lw, g, bonus, rwkv_ln_g[l], rwkv_ln_b[l])
        lambda_init = 0.8 - 0.6 * math.exp(-0.3 * l)
        y_d = _diff_attn(p_d, bias_df, diff_q_gain[l], diff_k_gain[l], diff_lambda[l],
                         diff_subln_g[l], lambda_init)
        y_m = _moba(p_m, bias_mb, moba_q_gain[l], moba_k_gain[l])
        w_router = jnp.zeros((D, LANES), F32).at[:, :n_route].set(
            jnp.concatenate([router_g_w[l], router_e_w[l]], axis=1))
        b_router = jnp.zeros((1, LANES), F32).at[0, :n_route].set(
            jnp.concatenate([router_g_b[l], router_e_b[l]]))
        x1, h2p, route = _outproj(y_r, y_d, y_m, x, g1, sc2, sh2, norm2_g[l], w_out[l].astype(BF16),
                                  w_router, b_router)
        x2 = _moe(h2p.reshape(T, D // 2), route.reshape(T, LANES), x1.reshape(T, D), g2,
                  moe_w1, moe_w3, moe_w2, S, l)
        x = x2.reshape(B, S, D)
    return x
```

```python
import functools
import math

import jax
import jax.numpy as jnp
from jax import lax
from jax.experimental import pallas as pl
from jax.experimental.pallas import tpu as pltpu

F32 = jnp.float32
BF16 = jnp.bfloat16
HIGHEST = lax.Precision.HIGHEST

D_MODEL = 1024
DEPTH = 4
HEAD_DIM = 64
R_HEADS = 4
R_WIDTH = 256
R_LORA_W = 32
R_LORA_A = 32
R_LORA_G = 64
R_COLS = 896
R_GN_EPS = 64e-5
DF_HEADS = 4
DF_V = 128
DF_WIDTH = 512
DF_COLS = 1536
MB_HEADS = 4
MB_WIDTH = 256
MB_COLS = 768
MB_BLOCK = 256
MB_TOPK = 3
IN_COLS = 3200
NUM_BUCKETS = 32
MAX_DISTANCE = 1024
N_GROUPS = 4
EXPERTS_PER_GROUP = 8
N_EXPERTS = 32
TOP_K_INNER = 2
EXPERT_HIDDEN = 512
MOE_BLOCK = 256
RMS_EPS = 1e-6

LANES = 128
SUBLANES = 8
ATT_TILE = 256
SAT_TILE = 5
MASK_TILE = 6
N_BIAS_TILES = 7
ATT_UNROLL = 2
Q_HALVES = 2
RW_CHUNK = 64
RW_CHUNKS_PER_STEP = 4
NEG_BIG = -1e30
LOG2E = math.log2(math.e)
QK_SCALE = HEAD_DIM ** -0.5 * LOG2E
VMEM_LIMIT = 48 * 1024 * 1024


def _cparams(sem):
    return pltpu.CompilerParams(dimension_semantics=sem, vmem_limit_bytes=VMEM_LIMIT)


def _mm(a, b):
    return jnp.dot(a.astype(BF16), b.astype(BF16), preferred_element_type=F32)


def _mm_nt(a, b):
    return lax.dot_general(a.astype(BF16), b.astype(BF16), (((1,), (1,)), ((), ())),
                           preferred_element_type=F32)


def _mm_tn(a, b):
    return lax.dot_general(a.astype(BF16), b.astype(BF16), (((0,), (0,)), ((), ())),
                           preferred_element_type=F32)


def _mm_hi(a, b):
    return jnp.dot(a, b, precision=HIGHEST, preferred_element_type=F32)


def _mm_3pass(a, b):
    a_hi = a.astype(BF16)
    a_lo = (a - a_hi.astype(F32)).astype(BF16)
    b_hi = b.astype(BF16)
    b_lo = (b - b_hi.astype(F32)).astype(BF16)
    m = a.shape[0]
    first = jnp.dot(jnp.concatenate([a_hi, a_lo], axis=0), b_hi, preferred_element_type=F32)
    return first[:m] + first[m:] + jnp.dot(a_hi, b_lo, preferred_element_type=F32)


def _split_bf16(x, terms):
    parts = []
    for _ in range(terms):
        p = x.astype(BF16)
        parts.append(p)
        x = x - p.astype(F32)
    return parts


def _mm_exact_rhs(a, b01, terms=2):
    m = a.shape[0]
    out = jnp.dot(jnp.concatenate(_split_bf16(a, terms), axis=0), b01.astype(BF16), preferred_element_type=F32)
    return functools.reduce(lambda x, y: x + y, [out[t * m:(t + 1) * m] for t in range(terms)])


def _mm_exact_lhs(a01, b, terms=3):
    n = b.shape[1]
    out = jnp.dot(a01.astype(BF16), jnp.concatenate(_split_bf16(b, terms), axis=1), preferred_element_type=F32)
    return functools.reduce(lambda x, y: x + y, [out[:, t * n:(t + 1) * n] for t in range(terms)])


def _adaln_kernel(c_ref, w_ref, b_ref, o_ref):
    c = c_ref[...]
    cs = c * jax.nn.sigmoid(c)
    o_ref[0] = _mm_hi(cs, w_ref[0]) + b_ref[0]


def _adaln(c, ada_w, ada_b):
    L, D, D6 = ada_w.shape
    B = c.shape[0]
    nj = D6 // D
    return pl.pallas_call(
        _adaln_kernel,
        grid=(L, nj),
        in_specs=[pl.BlockSpec((B, D), lambda l, j: (0, 0)),
                  pl.BlockSpec((1, D, D), lambda l, j: (l, 0, j)),
                  pl.BlockSpec((1, 1, D), lambda l, j: (l, 0, j))],
        out_specs=pl.BlockSpec((1, B, D), lambda l, j: (l, 0, j)),
        out_shape=jax.ShapeDtypeStruct((L, B, D6), F32),
        compiler_params=_cparams(("parallel", "parallel")),
        name="adaln",
    )(c, ada_w, ada_b.reshape(L, 1, D6))


def _ln_inproj_kernel(x_ref, g_ref, sc_ref, sh_ref, w_ref, pr_ref, pd_ref, pm_ref):
    x = x_ref[0]
    y = x * lax.rsqrt(jnp.mean(x * x, -1, keepdims=True) + RMS_EPS)
    h = ((y * g_ref[...]) * (1.0 + sc_ref[0]) + sh_ref[0]).astype(BF16)
    c1 = R_COLS
    c2 = R_COLS + DF_COLS
    pr_ref[0] = jnp.dot(h, w_ref[:, :c1], preferred_element_type=F32)
    pd_ref[0] = jnp.dot(h, w_ref[:, c1:c2], preferred_element_type=F32).astype(BF16)
    pm_ref[0] = jnp.dot(h, w_ref[:, c2:], preferred_element_type=F32).astype(BF16)


def _ln_inproj(x, g, sc, sh, w_bf16, tm=512):
    B, S, D = x.shape
    return pl.pallas_call(
        _ln_inproj_kernel,
        grid=(B, S // tm),
        in_specs=[pl.BlockSpec((1, tm, D), lambda b, i: (b, i, 0)),
                  pl.BlockSpec((1, D), lambda b, i: (0, 0)),
                  pl.BlockSpec((1, 1, D), lambda b, i: (b, 0, 0)),
                  pl.BlockSpec((1, 1, D), lambda b, i: (b, 0, 0)),
                  pl.BlockSpec((D, IN_COLS), lambda b, i: (0, 0))],
        out_specs=[pl.BlockSpec((1, tm, R_COLS), lambda b, i: (b, i, 0)),
                   pl.BlockSpec((1, tm, DF_COLS), lambda b, i: (b, i, 0)),
                   pl.BlockSpec((1, tm, MB_COLS), lambda b, i: (b, i, 0))],
        out_shape=[jax.ShapeDtypeStruct((B, S, R_COLS), F32),
                   jax.ShapeDtypeStruct((B, S, DF_COLS), BF16),
                   jax.ShapeDtypeStruct((B, S, MB_COLS), BF16)],
        compiler_params=_cparams(("parallel", "parallel")),
        name="ln_inproj",
    )(x, g.reshape(1, D), sc.reshape(B, 1, D), sh.reshape(B, 1, D), w_bf16)


def _head_ones(n):
    r = lax.broadcasted_iota(jnp.int32, (n, n), 0) // HEAD_DIM
    c = lax.broadcasted_iota(jnp.int32, (n, n), 1) // HEAD_DIM
    return (r == c).astype(F32)


def _rwkv_prep_kernel(p_ref, pp_ref, mu_ref, w0_ref, w2_ref, a0_ref, a2_ref, g2_ref, kk_ref, ka_ref,
                      rk_ref, r_o, k_o, v_o, kap_o, b_o, lw_o, g_o, bonus_o):
    i = pl.program_id(1)
    p = p_ref[0]
    prev_row = jnp.where(i > 0, pp_ref[0][7:8, :], 0.0)
    rows = lax.broadcasted_iota(jnp.int32, p.shape, 0)
    p_prev = jnp.where(rows == 0, prev_row, pltpu.roll(p, 1, 0))
    ps = p + (p_prev - p) * mu_ref[...]
    W = R_WIDTH
    r = ps[:, 0:W]
    k = ps[:, W:2 * W]
    v = ps[:, 2 * W:3 * W]
    c3 = 3 * W
    c4 = c3 + R_LORA_W
    c5 = c4 + R_LORA_A
    wd = ps[:, c3:c4]
    ad = ps[:, c4:c5]
    gd = ps[:, c5:R_COLS]
    z = -(w0_ref[...] + _mm_3pass(jnp.tanh(wd), w2_ref[...]))
    softplus = jnp.maximum(z, 0.0) + jnp.log(1.0 + jnp.exp(-jnp.abs(z)))
    lw_o[0] = -jnp.exp(-softplus - 0.5)
    a = jax.nn.sigmoid(a0_ref[...] + _mm_3pass(ad, a2_ref[...]))
    g_o[0] = _mm_3pass(jax.nn.sigmoid(gd), g2_ref[...])
    ones = _head_ones(W)
    kk = k * kk_ref[...]
    nrm = jnp.sqrt(_mm_exact_rhs(kk * kk, ones))
    kap = kk / jnp.maximum(nrm, 1e-12)
    k2 = k * (1.0 + (a - 1.0) * ka_ref[...])
    r_o[0] = r
    k_o[0] = k2
    v_o[0] = v
    kap_o[0] = kap
    b_o[0] = kap * a
    bonus_o[0] = _mm_exact_rhs(r * k2 * rk_ref[...], ones) * v


def _rwkv_prep(p_r, mu, w0, w2, a0, a2, g2, k_k, k_a, r_k, tm=512):
    B, S, _ = p_r.shape
    W = R_WIDTH
    row = lambda a: a.reshape(1, -1)
    full = lambda a: pl.BlockSpec(a.shape, lambda b, i: (0,) * a.ndim)
    ins = [row(mu), row(w0), w2, row(a0), a2, g2, row(k_k), row(k_a), row(r_k)]
    out_spec = pl.BlockSpec((1, tm, W), lambda b, i: (b, i, 0))
    return pl.pallas_call(
        _rwkv_prep_kernel,
        grid=(B, S // tm),
        in_specs=[pl.BlockSpec((1, tm, R_COLS), lambda b, i: (b, i, 0)),
                  pl.BlockSpec((1, 8, R_COLS), lambda b, i: (b, jnp.maximum(i * (tm // 8) - 1, 0), 0))]
                 + [full(a) for a in ins],
        out_specs=[out_spec] * 8,
        out_shape=[jax.ShapeDtypeStruct((B, S, W), F32)] * 8,
        compiler_params=_cparams(("parallel", "arbitrary")),
        name="rwkv_prep",
    )(p_r, p_r, *ins)


def _rwkv_chunk_kernel(r_ref, k_ref, v_ref, kap_ref, b_ref, lw_ref, g_ref, bonus_ref, lng_ref, lnb_ref,
                       o_ref, state):
    step = pl.program_id(1)
    C = RW_CHUNK
    NC = RW_CHUNKS_PER_STEP
    Dh = HEAD_DIM
    TM = NC * C

    @pl.when(step == 0)
    def _():
        state[...] = jnp.zeros_like(state)

    ti = lax.broadcasted_iota(jnp.int32, (C, C), 0)
    tj = lax.broadcasted_iota(jnp.int32, (C, C), 1)
    eye = (ti == tj).astype(F32)
    ti2 = lax.broadcasted_iota(jnp.int32, (C, 2 * C), 0)
    tj2 = lax.broadcasted_iota(jnp.int32, (C, 2 * C), 1) % C
    strict2 = tj2 < ti2
    incl2 = tj2 <= ti2

    bi = lax.broadcasted_iota(jnp.int32, (TM, TM), 0)
    bj = lax.broadcasted_iota(jnp.int32, (TM, TM), 1)
    chunk_tri = jnp.logical_and(bi // C == bj // C, bj <= bi).astype(F32)

    lw = lw_ref[0]
    cum = _mm_exact_lhs(chunk_tri, lw)
    cum_end = jnp.concatenate(
        [jnp.broadcast_to(cum[(c + 1) * C - 1:(c + 1) * C, :], (C, R_WIDTH)) for c in range(NC)], axis=0)
    e_pos = jnp.exp(cum)
    e_neg = jnp.exp(-cum)
    e_prev = jnp.exp(cum - lw)
    e_end = jnp.exp(cum_end - cum)
    g_end = jnp.exp(cum_end)
    r = r_ref[0]
    k = k_ref[0]
    v = v_ref[0]
    b = b_ref[0]
    rt = r * e_pos
    kapt = kap_ref[0] * e_prev
    bt = b * e_neg
    kt = k * e_neg
    bh = b * e_end
    kh = k * e_end

    items = [(c, h) for c in range(NC) for h in range(R_HEADS)]

    def tile(x, c, h):
        return x[c * C:(c + 1) * C, h * Dh:(h + 1) * Dh]

    a_top, a_bot = [], []
    for c, h in items:
        lhs = jnp.concatenate([tile(kapt, c, h), tile(rt, c, h)], axis=0)
        rhs = jnp.concatenate([tile(bt, c, h), tile(kt, c, h)], axis=0)
        a = _mm_nt(lhs, rhs)
        a_top.append(jnp.where(strict2, a[:C], 0.0))
        a_bot.append(jnp.where(incl2, a[C:], 0.0))
    xp = [-t[:, :C] for t in a_top]
    tinv = [eye + x for x in xp]
    xp = [_mm(x, x) for x in xp]
    for _ in range(4):
        prod = [_mm(jnp.concatenate([t, x], axis=0), x) for t, x in zip(tinv, xp)]
        tinv = [t + p[:C] for t, p in zip(tinv, prod)]
        xp = [p[C:] for p in prod]
    tinv = [t + _mm(t, x) for t, x in zip(tinv, xp)]
    vt = [tile(v, c, h) for c, h in items]
    lkv = [_mm(t[:, C:], vv) for t, vv in zip(a_top, vt)]
    w = [_mm(t, tile(kapt, c, h)) for t, (c, h) in zip(tinv, items)]
    ploc = [-_mm(t, x) for t, x in zip(tinv, lkv)]
    q_eff = [tile(rt, c, h) - _mm(ab[:, :C], ww) for ab, ww, (c, h) in zip(a_bot, w, items)]
    pv = [jnp.concatenate([p, vv], axis=0) for p, vv in zip(ploc, vt)]
    y_loc = [_mm(ab, x) for ab, x in zip(a_bot, pv)]
    g_mat = [eye * tile(g_end, c, h)[0:1, :] - _mm_tn(ww, tile(bh, c, h)) for ww, (c, h) in zip(w, items)]
    u_mat = [_mm_tn(x, jnp.concatenate([tile(bh, c, h), tile(kh, c, h)], axis=0))
             for x, (c, h) in zip(pv, items)]

    s_cur = [state[h] for h in range(R_HEADS)]
    ys = [[None] * NC for _ in range(R_HEADS)]
    for n, (c, h) in enumerate(items):
        ys[h][c] = _mm_nt(q_eff[n], s_cur[h]) + y_loc[n]
        s_cur[h] = _mm(s_cur[h], g_mat[n]) + u_mat[n]
    for h in range(R_HEADS):
        state[h] = s_cur[h]

    gate = g_ref[0]
    bonus = bonus_ref[0]
    lng = lng_ref[...]
    lnb = lnb_ref[...]
    lane = lax.broadcasted_iota(jnp.int32, (Dh, R_WIDTH), 1)
    sub = lax.broadcasted_iota(jnp.int32, (Dh, R_WIDTH), 0)
    out = jnp.zeros((TM, R_WIDTH), F32)
    for h in range(R_HEADS):
        sl = slice(h * Dh, (h + 1) * Dh)
        y = jnp.concatenate(ys[h], axis=0)
        mean = jnp.mean(y, -1, keepdims=True)
        yc = y - mean
        var = jnp.mean(yc * yc, -1, keepdims=True)
        yn = yc * lax.rsqrt(var + R_GN_EPS)
        yfin = (yn * lng[:, sl] + lnb[:, sl] + bonus[:, sl]) * gate[:, sl]
        place = (lane == sub + h * Dh).astype(BF16)
        out = out + jnp.dot(yfin.astype(BF16), place, preferred_element_type=F32)
    o_ref[0] = out.astype(BF16)


def _rwkv_chunk(r, k, v, kap, b, lw, g, bonus, ln_g, ln_b):
    B, S, W = r.shape
    tm = RW_CHUNK * RW_CHUNKS_PER_STEP
    spec = pl.BlockSpec((1, tm, W), lambda bb, c: (bb, c, 0))
    vec = pl.BlockSpec((1, W), lambda bb, c: (0, 0))
    return pl.pallas_call(
        _rwkv_chunk_kernel,
        grid=(B, S // tm),
        in_specs=[spec] * 8 + [vec, vec],
        out_specs=spec,
        out_shape=jax.ShapeDtypeStruct((B, S, W), BF16),
        scratch_shapes=[pltpu.VMEM((R_HEADS, HEAD_DIM, HEAD_DIM), F32)],
        compiler_params=_cparams(("parallel", "arbitrary")),
        name="rwkv_chunk",
    )(r, k, v, kap, b, lw, g, bonus, ln_g.reshape(1, W), ln_b.reshape(1, W))


def _t5_bucket(dist):
    n = jnp.maximum(dist, 0)
    max_exact = NUM_BUCKETS // 2
    nf = jnp.maximum(n, 1).astype(F32)
    large = max_exact + (jnp.log(nf / max_exact) / math.log(MAX_DISTANCE / max_exact)
                         * (NUM_BUCKETS - max_exact)).astype(jnp.int32)
    large = jnp.minimum(large, NUM_BUCKETS - 1)
    return jnp.where(n < max_exact, n, large)


def _bias_tiles(tbl):
    T = ATT_TILE
    assert (SAT_TILE - 1) * T + 1 >= MAX_DISTANCE
    nh = tbl.shape[1]
    o = jnp.arange(SAT_TILE + 1)[:, None, None]
    i = jnp.arange(T)[None, :, None]
    j = jnp.arange(T)[None, None, :]
    dist = o * T + i - j
    onehot = (_t5_bucket(dist)[None] == jnp.arange(NUM_BUCKETS)[:, None, None, None]).astype(F32)
    tiles = jnp.dot(tbl.astype(F32).T, onehot.reshape(NUM_BUCKETS, -1), precision=HIGHEST)
    tiles = jnp.where((dist >= 0)[None], tiles.reshape(nh, SAT_TILE + 1, T, T) * LOG2E, NEG_BIG)
    return jnp.concatenate([tiles, jnp.full((nh, 1, T, T), NEG_BIG, F32)], axis=1)


def _half_masks(shape):
    lane = lax.broadcasted_iota(jnp.int32, shape, len(shape) - 1)
    lo = lane < HEAD_DIM
    return lo, jnp.logical_not(lo)


def _rms_halves(x, gain2):
    lo, hi = _half_masks(x.shape)
    sq = x * x
    s_lo = jnp.sum(jnp.where(lo, sq, 0.0), -1, keepdims=True)
    s_hi = jnp.sum(jnp.where(hi, sq, 0.0), -1, keepdims=True)
    inv = jnp.where(lo, lax.rsqrt(s_lo / HEAD_DIM + RMS_EPS), lax.rsqrt(s_hi / HEAD_DIM + RMS_EPS))
    return x * inv * gain2


def _lane_chunks(x):
    return [x[:, c * LANES:(c + 1) * LANES] for c in range(x.shape[1] // LANES)]


def _flash(q2, kn, v_ref, bias_tile, qi, s_bufs, m_s, l_s, acc_s):
    T = ATT_TILE
    U = ATT_UNROLL
    H = Q_HALVES
    n_iter = (qi * H + H - 1) // U + 1

    def window(i):
        return pl.ds(pl.multiple_of(jnp.minimum(i, n_iter - 1) * (U * T), U * T), U * T)

    def produce(i, dst):
        s = lax.dot_general(q2, kn[window(i), :], (((1,), (1,)), ((), ())), preferred_element_type=F32)
        idx = {}
        for d in range(-(U - 1), H):
            off = qi * H - i * U + d
            idx[d] = jnp.where(off < 0, MASK_TILE, jnp.minimum(off, SAT_TILE))
        bias = jnp.concatenate(
            [jnp.concatenate([bias_tile(g, idx[a - u]) for u in range(U)], axis=1)
             for g in range(2) for a in range(H)], axis=0)
        dst[...] = s + bias

    def consume(i, src):
        parts = _lane_chunks(src[...])
        m_old = m_s[...]
        blk_max = jnp.max(functools.reduce(jnp.maximum, parts), -1, keepdims=True)
        m_new = jnp.maximum(m_old, jnp.broadcast_to(blk_max, m_old.shape))
        alpha = jnp.exp2(m_old - m_new)
        ps = [jnp.exp2(part - m_new) for part in parts]
        l_s[...] = alpha * l_s[...] + functools.reduce(lambda a, b: a + b, ps)
        p = jnp.concatenate(ps, axis=1).astype(BF16)
        acc_s[...] = alpha * acc_s[...] + jnp.dot(p, v_ref[0, window(i), :], preferred_element_type=F32)
        m_s[...] = m_new

    m_s[...] = jnp.full_like(m_s, -jnp.inf)
    l_s[...] = jnp.zeros_like(l_s)
    acc_s[...] = jnp.zeros_like(acc_s)
    produce(0, s_bufs[0])

    def body(i, carry):
        for par in range(2):
            @pl.when(i % 2 == par)
            def _():
                produce(i + 1, s_bufs[1 - par])
                consume(i, s_bufs[par])
        return carry

    lax.fori_loop(0, n_iter, body, 0)
    return acc_s[...] / jnp.sum(l_s[...], -1, keepdims=True)


def _diff_attn_kernel(q_ref, k_ref, v_ref, bias_ref, qg_ref, kg_ref, lam_ref, sg_ref, o_ref,
                      kn, s_a, s_b, m_s, l_s, acc_s, *, lambda_init):
    qi = pl.program_id(2)
    T = ATT_TILE
    nkb = k_ref.shape[1] // T

    @pl.when(qi == 0)
    def _():
        def body(j, carry):
            rows = pl.ds(pl.multiple_of(j * T, T), T)
            kn[rows, :] = _rms_halves(k_ref[0, rows, :].astype(F32), kg_ref[...]).astype(BF16)
            return carry
        lax.fori_loop(0, nkb, body, 0)

    q = _rms_halves(q_ref[0].astype(F32), qg_ref[...]) * QK_SCALE
    tq = q.shape[0]
    lo, hi = _half_masks(q.shape)
    q2 = jnp.concatenate([jnp.where(lo, q, 0.0), jnp.where(hi, q, 0.0)], axis=0).astype(BF16)

    def bias_tile(g, idx):
        return bias_ref[0, idx]

    o = _flash(q2, kn, v_ref, bias_tile, qi, (s_a, s_b), m_s, l_s, acc_s)
    lam = lam_ref[...]
    lam_full = (jnp.exp(jnp.sum(lam[0:1] * lam[1:2], -1, keepdims=True))
                - jnp.exp(jnp.sum(lam[2:3] * lam[3:4], -1, keepdims=True)) + lambda_init)
    out = o[:tq] - lam_full * o[tq:]
    out = out * lax.rsqrt(jnp.mean(out * out, -1, keepdims=True) + RMS_EPS) * sg_ref[...]
    o_ref[0] = (out * (1.0 - lambda_init)).astype(BF16)


def _diff_attn(p_d, bias_tiles, q_gain, k_gain, lam, subln_g, lambda_init):
    B, S, _ = p_d.shape
    T = ATT_TILE
    H = DF_HEADS
    TQ = Q_HALVES * T
    R = 2 * TQ
    assert S % (ATT_UNROLL * T) == 0 and S % TQ == 0
    gain2 = lambda g: jnp.concatenate([g, g]).reshape(1, 2 * HEAD_DIM)
    kern = functools.partial(_diff_attn_kernel, lambda_init=lambda_init)
    return pl.pallas_call(
        kern,
        grid=(B, H, S // TQ),
        in_specs=[pl.BlockSpec((1, TQ, LANES), lambda b, h, i: (b, i, h)),
                  pl.BlockSpec((1, S, LANES), lambda b, h, i: (b, 0, H + h)),
                  pl.BlockSpec((1, S, LANES), lambda b, h, i: (b, 0, 2 * H + h)),
                  pl.BlockSpec((1, N_BIAS_TILES, T, T), lambda b, h, i: (h, 0, 0, 0)),
                  pl.BlockSpec((1, LANES), lambda b, h, i: (0, 0)),
                  pl.BlockSpec((1, LANES), lambda b, h, i: (0, 0)),
                  pl.BlockSpec((4, HEAD_DIM), lambda b, h, i: (0, 0)),
                  pl.BlockSpec((1, DF_V), lambda b, h, i: (0, 0))],
        out_specs=pl.BlockSpec((1, TQ, DF_V), lambda b, h, i: (b, i, h)),
        out_shape=jax.ShapeDtypeStruct((B, S, DF_WIDTH), BF16),
        scratch_shapes=[pltpu.VMEM((S, LANES), BF16),
                        pltpu.VMEM((R, ATT_UNROLL * T), F32),
                        pltpu.VMEM((R, ATT_UNROLL * T), F32),
                        pltpu.VMEM((R, LANES), F32),
                        pltpu.VMEM((R, LANES), F32),
                        pltpu.VMEM((R, DF_V), F32)],
        compiler_params=_cparams(("parallel", "parallel", "arbitrary")),
        name="diff_attn",
    )(p_d, p_d, p_d, bias_tiles, gain2(q_gain), gain2(k_gain), lam, subln_g.reshape(1, DF_V))


def _moba_kernel(q_ref, k_ref, v_ref, bias_ref, qg_ref, kg_ref, o_ref, kn, kmean, s_a, s_b, m_s, l_s,
                 acc_s):
    qi = pl.program_id(2)
    T = ATT_TILE
    nkb = k_ref.shape[1] // T

    @pl.when(qi == 0)
    def _():
        kmean[...] = jnp.zeros_like(kmean)
        lane = lax.broadcasted_iota(jnp.int32, (T, LANES), 1)

        def body(j, carry):
            rows = pl.ds(pl.multiple_of(j * T, T), T)
            kf = _rms_halves(k_ref[0, rows, :].astype(F32), kg_ref[...])
            kn[rows, :] = jnp.concatenate([kf.astype(BF16), (lane == j).astype(BF16)], axis=1)
            kmean[pl.ds(j, 1), :] = jnp.mean(kf, 0, keepdims=True)
            return carry
        lax.fori_loop(0, nkb, body, 0)

    q = _rms_halves(q_ref[0].astype(F32), qg_ref[...])
    tq = q.shape[0]
    lo, hi = _half_masks(q.shape)
    q2f = jnp.concatenate([jnp.where(lo, q, 0.0), jnp.where(hi, q, 0.0)], axis=0)

    nkp = kmean.shape[0]
    km = kmean[...]
    km_hi = km.astype(BF16)
    km_lo = (km - km_hi.astype(F32)).astype(BF16)
    q_hi = q2f.astype(BF16)
    q_lo = (q2f - q_hi.astype(F32)).astype(BF16)
    nt = (((1,), (1,)), ((), ()))
    g_hi = lax.dot_general(jnp.concatenate([km_hi, km_lo], axis=0), q_hi, nt, preferred_element_type=F32)
    gate = g_hi[:nkp] + g_hi[nkp:] + lax.dot_general(km_hi, q_lo, nt, preferred_element_type=F32)
    blk = lax.broadcasted_iota(jnp.int32, gate.shape, 0).astype(F32)
    col = lax.broadcasted_iota(jnp.int32, gate.shape, 1)
    own = (qi * Q_HALVES + (col % tq) // T).astype(F32)
    past = blk < own
    gate = jnp.where(past, gate, -jnp.inf)
    pen_t = jnp.where(past, NEG_BIG, 0.0)
    for _ in range(MB_TOPK):
        mx = jnp.max(gate, 0, keepdims=True)
        first = jnp.min(jnp.where(gate == mx, blk, float(nkp)), 0, keepdims=True)
        pick = jnp.logical_and(blk == first, mx > -jnp.inf)
        pen_t = jnp.where(pick, 0.0, pen_t)
        gate = jnp.where(pick, -jnp.inf, gate)
    pen = jnp.concatenate([pen_t, jnp.zeros((LANES - nkp, 2 * tq), F32)], axis=0).T
    q2 = jnp.concatenate([(q2f * QK_SCALE).astype(BF16), pen.astype(BF16)], axis=1)

    def bias_tile(g, idx):
        return bias_ref[g, idx]

    o = _flash(q2, kn, v_ref, bias_tile, qi, (s_a, s_b), m_s, l_s, acc_s)
    lo_o, _ = _half_masks((tq, LANES))
    o_ref[0] = jnp.where(lo_o, o[:tq], o[tq:]).astype(BF16)


def _moba(p_m, bias_tiles, q_gain, k_gain):
    B, S, _ = p_m.shape
    T = ATT_TILE
    HP = MB_HEADS // 2
    TQ = Q_HALVES * T
    R = 2 * TQ
    assert T == MB_BLOCK and S // T <= LANES and S % (ATT_UNROLL * T) == 0 and S % TQ == 0
    gain2 = lambda g: jnp.concatenate([g, g]).reshape(1, 2 * HEAD_DIM)
    return pl.pallas_call(
        _moba_kernel,
        grid=(B, HP, S // TQ),
        in_specs=[pl.BlockSpec((1, TQ, LANES), lambda b, h, i: (b, i, h)),
                  pl.BlockSpec((1, S, LANES), lambda b, h, i: (b, 0, HP + h)),
                  pl.BlockSpec((1, S, LANES), lambda b, h, i: (b, 0, 2 * HP + h)),
                  pl.BlockSpec((2, N_BIAS_TILES, T, T), lambda b, h, i: (h, 0, 0, 0)),
                  pl.BlockSpec((1, LANES), lambda b, h, i: (0, 0)),
                  pl.BlockSpec((1, LANES), lambda b, h, i: (0, 0))],
        out_specs=pl.BlockSpec((1, TQ, LANES), lambda b, h, i: (b, i, h)),
        out_shape=jax.ShapeDtypeStruct((B, S, MB_WIDTH), BF16),
        scratch_shapes=[pltpu.VMEM((S, 2 * LANES), BF16),
                        pltpu.VMEM((-(-(S // T) // 8) * 8, LANES), F32),
                        pltpu.VMEM((R, ATT_UNROLL * T), F32),
                        pltpu.VMEM((R, ATT_UNROLL * T), F32),
                        pltpu.VMEM((R, LANES), F32),
                        pltpu.VMEM((R, LANES), F32),
                        pltpu.VMEM((R, LANES), F32)],
        compiler_params=_cparams(("parallel", "parallel", "arbitrary")),
        name="moba",
    )(p_m, p_m, p_m, bias_tiles, gain2(q_gain), gain2(k_gain))


def _pack_pairs(x):
    n = x.shape[1] // 2
    hi = pltpu.bitcast(x[:, :n].astype(BF16).astype(F32), jnp.uint32)
    lo = pltpu.bitcast(x[:, n:].astype(BF16).astype(F32), jnp.uint32)
    return hi | (lo >> 16)


def _unpack_pairs(p):
    hi = pltpu.bitcast(p & jnp.uint32(0xFFFF0000), F32)
    lo = pltpu.bitcast(p << 16, F32)
    return jnp.concatenate([hi, lo], axis=1)


def _outproj_kernel(yr_ref, yd_ref, ym_ref, x_ref, g1_ref, sc_ref, sh_ref, ng_ref, w_ref, wr_ref, br_ref,
                    x1_ref, h2_ref, route_ref):
    c1 = R_WIDTH
    c2 = R_WIDTH + DF_WIDTH
    mix = (jnp.dot(yr_ref[0], w_ref[:c1, :], preferred_element_type=F32)
           + jnp.dot(yd_ref[0], w_ref[c1:c2, :], preferred_element_type=F32)
           + jnp.dot(ym_ref[0], w_ref[c2:, :], preferred_element_type=F32))
    x1 = x_ref[0] + g1_ref[0] * mix
    x1_ref[0] = x1
    y = x1 * lax.rsqrt(jnp.mean(x1 * x1, -1, keepdims=True) + RMS_EPS)
    h2 = (y * ng_ref[...]) * (1.0 + sc_ref[0]) + sh_ref[0]
    h2_ref[0] = _pack_pairs(h2)

    logits = _mm_3pass(h2, wr_ref[...]) + br_ref[...]
    lane = lax.broadcasted_iota(jnp.int32, logits.shape, 1).astype(F32)
    far = float(LANES)
    is_g = lane < N_GROUPS
    gl = jnp.where(is_g, logits, -jnp.inf)
    gmax = jnp.max(gl, -1, keepdims=True)
    g_idx = jnp.min(jnp.where(gl == gmax, lane, far), -1, keepdims=True)
    pg_top = 1.0 / jnp.sum(jnp.where(is_g, jnp.exp(logits - gmax), 0.0), -1, keepdims=True)
    e_lo = N_GROUPS + EXPERTS_PER_GROUP * g_idx
    in_grp = jnp.logical_and(lane >= e_lo, lane < e_lo + EXPERTS_PER_GROUP)
    el = jnp.where(in_grp, logits, -jnp.inf)
    m1 = jnp.max(el, -1, keepdims=True)
    i1 = jnp.min(jnp.where(el == m1, lane, far), -1, keepdims=True)
    el2 = jnp.where(lane == i1, -jnp.inf, el)
    m2 = jnp.max(el2, -1, keepdims=True)
    i2 = jnp.min(jnp.where(el2 == m2, lane, far), -1, keepdims=True)
    e2 = jnp.exp(m2 - m1)
    gate1 = pg_top / (1.0 + e2)
    gate2 = gate1 * e2
    route = jnp.where(lane == 0, gate1, 0.0)
    route = jnp.where(lane == 1, gate2, route)
    route = jnp.where(lane == 2, i1 - N_GROUPS, route)
    route = jnp.where(lane == 3, i2 - N_GROUPS, route)
    route_ref[0] = route


def _outproj(y_r, y_d, y_m, x, g1, sc2, sh2, norm2_g, w_out_bf16, w_router, b_router, tm=512):
    B, S, D = x.shape
    vec = pl.BlockSpec((1, 1, D), lambda b, i: (b, 0, 0))
    tok = lambda w: pl.BlockSpec((1, tm, w), lambda b, i: (b, i, 0))
    full = lambda a: pl.BlockSpec(a.shape, lambda b, i: (0,) * a.ndim)
    return pl.pallas_call(
        _outproj_kernel,
        grid=(B, S // tm),
        in_specs=[tok(R_WIDTH), tok(DF_WIDTH), tok(MB_WIDTH), tok(D), vec, vec, vec,
                  pl.BlockSpec((1, D), lambda b, i: (0, 0)), full(w_out_bf16), full(w_router), full(b_router)],
        out_specs=[tok(D), tok(D // 2), tok(LANES)],
        out_shape=[jax.ShapeDtypeStruct((B, S, D), F32),
                   jax.ShapeDtypeStruct((B, S, D // 2), jnp.uint32),
                   jax.ShapeDtypeStruct((B, S, LANES), F32)],
        compiler_params=_cparams(("parallel", "parallel")),
        name="outproj_router",
    )(y_r, y_d, y_m, x, g1.reshape(B, 1, D), sc2.reshape(B, 1, D), sh2.reshape(B, 1, D),
      norm2_g.reshape(1, D), w_out_bf16, w_router, b_router)


def _rank_kernel(route_ref, rank_ref, cnt_ref, carry):
    i = pl.program_id(0)

    @pl.when(i == 0)
    def _():
        carry[...] = jnp.zeros_like(carry)

    rt = route_ref[...]
    tm = rt.shape[0]
    lane = lax.broadcasted_iota(jnp.int32, rt.shape, 1).astype(F32)
    oh0 = (lane == rt[:, 2:3]).astype(F32)
    oh1 = (lane == rt[:, 3:4]).astype(F32)
    oh = oh0 + oh1
    row = lax.broadcasted_iota(jnp.int32, (tm, tm), 0)
    col = lax.broadcasted_iota(jnp.int32, (tm, tm), 1)
    before = jnp.dot((col < row).astype(BF16), oh.astype(BF16), preferred_element_type=F32) + carry[...]
    r0 = jnp.sum(oh0 * before, -1, keepdims=True)
    r1 = jnp.sum(oh1 * before, -1, keepdims=True)
    rank_ref[...] = jnp.where(lane == 0, r0, jnp.where(lane == 1, r1, 0.0))
    carry[...] = carry[...] + jnp.sum(oh, 0, keepdims=True)
    cnt_ref[...] = carry[...]


def _ranks(route, tm=512):
    T = route.shape[0]
    return pl.pallas_call(
        _rank_kernel,
        grid=(T // tm,),
        in_specs=[pl.BlockSpec((tm, LANES), lambda i: (i, 0))],
        out_specs=[pl.BlockSpec((tm, LANES), lambda i: (i, 0)), pl.BlockSpec((1, LANES), lambda i: (0, 0))],
        out_shape=[jax.ShapeDtypeStruct((T, LANES), F32), jax.ShapeDtypeStruct((1, LANES), F32)],
        scratch_shapes=[pltpu.VMEM((1, LANES), F32)],
        compiler_params=_cparams(("arbitrary",)),
        name="moe_ranks",
    )(route)


def _dispatch_kernel(dest_ref, h_ref, buf_in, buf_out, sem):
    del buf_in
    tm = h_ref.shape[0]

    def row_copy(t, slot):
        return pltpu.make_async_copy(h_ref.at[pl.ds(t, 1)], buf_out.at[pl.ds(slot, 1)], sem)

    def issue(g, carry):
        base = pl.multiple_of(g * SUBLANES, SUBLANES)
        for r in range(SUBLANES):
            for k in range(TOP_K_INNER):
                row_copy(base + r, dest_ref[TOP_K_INNER * (base + r) + k]).start(priority=k % 2)
        return carry

    def drain(g, carry):
        for _ in range(SUBLANES * TOP_K_INNER):
            row_copy(0, 0).wait()
        return carry

    lax.fori_loop(0, tm // SUBLANES, issue, 0)
    lax.fori_loop(0, tm // SUBLANES, drain, 0)


def _dispatch(h2p, dest_flat, n_slots, tm=512):
    T, W = h2p.shape
    buf0 = jnp.zeros((n_slots, W), jnp.uint32)
    return pl.pallas_call(
        _dispatch_kernel,
        grid=(T // tm,),
        in_specs=[pl.BlockSpec((TOP_K_INNER * tm,), lambda i: (i,), memory_space=pltpu.SMEM),
                  pl.BlockSpec((tm, W), lambda i: (i, 0)),
                  pl.BlockSpec(memory_space=pl.ANY)],
        out_specs=pl.BlockSpec(memory_space=pl.ANY),
        out_shape=jax.ShapeDtypeStruct((n_slots, W), jnp.uint32),
        scratch_shapes=[pltpu.SemaphoreType.DMA],
        input_output_aliases={2: 0},
        compiler_params=_cparams(("arbitrary",)),
        name="moe_dispatch",
    )(dest_flat, h2p, buf0)


def _expert_kernel(be_ref, slot_ref, nxt_ref, nu_ref, x_ref, w1_hbm, w3_hbm, w2_hbm, o_ref,
                   w1_s, w3_s, w2_s, sems, *, layer):
    i = pl.program_id(0)
    used = i < nu_ref[0]
    e = be_ref[i]
    slot = slot_ref[i]
    run_start = jnp.logical_or(i == 0, e != be_ref[jnp.maximum(i - 1, 0)])

    def fetch(expert, s):
        return [pltpu.make_async_copy(w1_hbm.at[layer, expert], w1_s.at[s], sems.at[s, 0]),
                pltpu.make_async_copy(w3_hbm.at[layer, expert], w3_s.at[s], sems.at[s, 1]),
                pltpu.make_async_copy(w2_hbm.at[layer, expert], w2_s.at[s], sems.at[s, 2])]

    @pl.when(jnp.logical_and(used, run_start))
    def _():
        @pl.when(i == 0)
        def _():
            for cp in fetch(e, slot):
                cp.start()
        for cp in fetch(e, slot):
            cp.wait()
        nxt = nxt_ref[i]

        @pl.when(nxt >= 0)
        def _():
            for cp in fetch(nxt, 1 - slot):
                cp.start()

    @pl.when(used)
    def _():
        x = _unpack_pairs(x_ref[...]).astype(BF16)
        h1 = jnp.dot(x, w1_s[slot].astype(BF16), preferred_element_type=F32)
        h3 = jnp.dot(x, w3_s[slot].astype(BF16), preferred_element_type=F32)
        h = (h1 * jax.nn.sigmoid(h1) * h3).astype(BF16)
        o_ref[...] = _pack_pairs(jnp.dot(h, w2_s[slot].astype(BF16), preferred_element_type=F32))

    @pl.when(jnp.logical_not(used))
    def _():
        o_ref[...] = jnp.zeros_like(o_ref)


def _experts(buf, blk_e, counts, n_used, w1, w3, w2, layer):
    n_slots, W = buf.shape
    _, E, D, Hd = w1.shape
    nb = n_slots // MOE_BLOCK
    change = jnp.concatenate([jnp.ones((1,), jnp.int32), (blk_e[1:] != blk_e[:-1]).astype(jnp.int32)])
    slot = (jnp.cumsum(change) - 1) % 2
    ids = jnp.arange(E, dtype=jnp.int32)
    later = jnp.logical_and(ids[None, :] > ids[:, None], counts[None, :] > 0)
    nxt_e = jnp.min(jnp.where(later, ids[None, :], E), axis=1)
    nxt_e = jnp.where(nxt_e == E, -1, nxt_e)
    nxt = jnp.sum(jnp.where(blk_e[:, None] == ids[None, :], nxt_e[None, :], 0), axis=1)
    any_spec = pl.BlockSpec(memory_space=pl.ANY)
    grid_spec = pltpu.PrefetchScalarGridSpec(
        num_scalar_prefetch=4,
        grid=(nb,),
        in_specs=[pl.BlockSpec((MOE_BLOCK, W), lambda i, *_: (i, 0)), any_spec, any_spec, any_spec],
        out_specs=pl.BlockSpec((MOE_BLOCK, W), lambda i, *_: (i, 0)),
        scratch_shapes=[pltpu.VMEM((2, D, Hd), F32), pltpu.VMEM((2, D, Hd), F32),
                        pltpu.VMEM((2, Hd, D), F32), pltpu.SemaphoreType.DMA((2, 3))],
    )
    return pl.pallas_call(
        functools.partial(_expert_kernel, layer=layer),
        grid_spec=grid_spec,
        out_shape=jax.ShapeDtypeStruct((n_slots, W), jnp.uint32),
        compiler_params=_cparams(("arbitrary",)),
        name="moe_experts",
    )(blk_e, slot.astype(jnp.int32), nxt.astype(jnp.int32), n_used, buf, w1, w3, w2)


def _combine_kernel(dest_ref, dest_nxt_ref, yb_ref, route_ref, x_ref, g2_ref, *rest, next_ln):
    if next_ln:
        ng_ref, sc_ref, sh_ref, w_ref, o_ref, pr_ref, pd_ref, pm_ref, rows, sems = rest
    else:
        o_ref, rows, sems = rest
    i = pl.program_id(0)
    n = pl.num_programs(0)
    tm = x_ref.shape[0]
    cur = i % 2

    def row_copy(buf, t, k, src_row):
        return pltpu.make_async_copy(yb_ref.at[pl.ds(src_row, 1)], rows.at[buf, k, pl.ds(t, 1)], sems.at[buf])

    def start_tile(d_ref, buf):
        def group(g, carry):
            base = pl.multiple_of(g * SUBLANES, SUBLANES)
            for r in range(SUBLANES):
                for k in range(TOP_K_INNER):
                    row_copy(buf, base + r, k, d_ref[TOP_K_INNER * (base + r) + k]).start(priority=k % 2)
            return carry
        lax.fori_loop(0, tm // SUBLANES, group, 0)

    def wait_tile(buf):
        def group(g, carry):
            for _ in range(SUBLANES * TOP_K_INNER):
                row_copy(buf, 0, 0, 0).wait()
            return carry
        lax.fori_loop(0, tm // SUBLANES, group, 0)

    @pl.when(i == 0)
    def _():
        start_tile(dest_ref, 0)

    @pl.when(i + 1 < n)
    def _():
        start_tile(dest_nxt_ref, 1 - cur)

    wait_tile(cur)
    rt = route_ref[...]
    y = rt[:, 0:1] * _unpack_pairs(rows[cur, 0]) + rt[:, 1:2] * _unpack_pairs(rows[cur, 1])
    x2 = x_ref[...] + g2_ref[0] * y
    o_ref[...] = x2
    if next_ln:
        yn = x2 * lax.rsqrt(jnp.mean(x2 * x2, -1, keepdims=True) + RMS_EPS)
        h = ((yn * ng_ref[...]) * (1.0 + sc_ref[0]) + sh_ref[0]).astype(BF16)
        c1 = R_COLS
        c2 = R_COLS + DF_COLS
        pr_ref[...] = jnp.dot(h, w_ref[:, :c1], preferred_element_type=F32)
        pd_ref[...] = jnp.dot(h, w_ref[:, c1:c2], preferred_element_type=F32).astype(BF16)
        pm_ref[...] = jnp.dot(h, w_ref[:, c2:], preferred_element_type=F32).astype(BF16)


def _combine(yb, dest_flat, route, x1, g2, seq, next_ln=None, tm=512):
    T, D = x1.shape
    W = yb.shape[1]
    per = seq // tm
    B = T // seq
    n = T // tm
    tok = lambda w: pl.BlockSpec((tm, w), lambda i: (i, 0))
    vec = pl.BlockSpec((1, 1, D), lambda i: (i // per, 0, 0))
    in_specs = [pl.BlockSpec((TOP_K_INNER * tm,), lambda i: (i,), memory_space=pltpu.SMEM),
                pl.BlockSpec((TOP_K_INNER * tm,), lambda i: (jnp.minimum(i + 1, n - 1),),
                             memory_space=pltpu.SMEM),
                pl.BlockSpec(memory_space=pl.ANY), tok(LANES), tok(D), vec]
    args = [dest_flat, dest_flat, yb, route, x1, g2.reshape(B, 1, D)]
    out_specs = [tok(D)]
    out_shape = [jax.ShapeDtypeStruct((T, D), F32)]
    if next_ln is not None:
        ng, sc, sh, w_bf16 = next_ln
        in_specs += [pl.BlockSpec((1, D), lambda i: (0, 0)), vec, vec,
                     pl.BlockSpec((D, IN_COLS), lambda i: (0, 0))]
        args += [ng.reshape(1, D), sc.reshape(B, 1, D), sh.reshape(B, 1, D), w_bf16]
        out_specs += [tok(R_COLS), tok(DF_COLS), tok(MB_COLS)]
        out_shape += [jax.ShapeDtypeStruct((T, R_COLS), F32), jax.ShapeDtypeStruct((T, DF_COLS), BF16),
                      jax.ShapeDtypeStruct((T, MB_COLS), BF16)]
    return pl.pallas_call(
        functools.partial(_combine_kernel, next_ln=next_ln is not None),
        grid=(n,),
        in_specs=in_specs,
        out_specs=out_specs,
        out_shape=out_shape,
        scratch_shapes=[pltpu.VMEM((2, TOP_K_INNER, tm, W), jnp.uint32), pltpu.SemaphoreType.DMA((2,))],
        compiler_params=_cparams(("arbitrary",)),
        name="moe_combine_ln" if next_ln is not None else "moe_combine",
    )(*args)


def _moe(h2p, route, x1, g2, w1, w3, w2, seq, layer, next_ln=None):
    T, D = x1.shape
    A = T * TOP_K_INNER
    n_blocks = -(-(A + N_EXPERTS * (MOE_BLOCK - 1)) // MOE_BLOCK)
    rank, cnt = _ranks(route)
    counts = cnt[0, :N_EXPERTS].astype(jnp.int32)
    padded = (counts + MOE_BLOCK - 1) // MOE_BLOCK * MOE_BLOCK
    pad_end = jnp.cumsum(padded)
    pad_start = pad_end - padded
    experts = route[:, 2:2 + TOP_K_INNER].astype(jnp.int32)
    is_e = experts[:, :, None] == jnp.arange(N_EXPERTS, dtype=jnp.int32)
    dest = (jnp.sum(jnp.where(is_e, pad_start, 0), axis=-1) + rank[:, :TOP_K_INNER].astype(jnp.int32)).reshape(A)
    blk_first = jnp.arange(n_blocks, dtype=jnp.int32) * MOE_BLOCK
    blk_e = jnp.minimum(jnp.sum((pad_end[None, :] <= blk_first[:, None]).astype(jnp.int32), axis=1),
                        N_EXPERTS - 1)
    n_used = (pad_end[-1:] // MOE_BLOCK).astype(jnp.int32)
    buf = _dispatch(h2p, dest, n_blocks * MOE_BLOCK)
    yb = _experts(buf, blk_e, counts, n_used, w1, w3, w2, layer)
    return _combine(yb, dest, route, x1, g2, seq, next_ln)


def kernel(x, c, ada_w, ada_b, norm1_g, norm2_g, w_in, w_out, rwkv_mu, rwkv_w0, rwkv_w2, rwkv_a0, rwkv_a2, rwkv_g2, rwkv_kk, rwkv_ka, rwkv_rk, rwkv_ln_g, rwkv_ln_b, diff_q_gain, diff_k_gain, diff_lambda, diff_subln_g, moba_q_gain, moba_k_gain, rel_bias, router_g_w, router_g_b, router_e_w, router_e_b, moe_w1, moe_w3, moe_w2):
    B, S, D = x.shape
    T = B * S
    mods = _adaln(c, ada_w, ada_b)
    bias_df = _bias_tiles(rel_bias[:, :DF_HEADS])
    bias_mb = _bias_tiles(rel_bias[:, DF_HEADS:])
    n_route = N_GROUPS + N_EXPERTS
    mod = lambda l, j: mods[l, :, j * D:(j + 1) * D]
    p_r, p_d, p_m = _ln_inproj(x, norm1_g[0], mod(0, 1), mod(0, 0), w_in[0].astype(BF16))
    for l in range(DEPTH):
        g1, sh2, sc2, g2 = mod(l, 2), mod(l, 3), mod(l, 4), mod(l, 5)
        r, k, v, kap, b, lw, g, bonus = _rwkv_prep(p_r, rwkv_mu[l], rwkv_w0[l], rwkv_w2[l], rwkv_a0[l],
                                                   rwkv_a2[l], rwkv_g2[l], rwkv_kk[l], rwkv_ka[l],
                                                   rwkv_rk[l])
        y_r = _rwkv_chunk(r, k, v, kap, b, lw, g, bonus, rwkv_ln_g[l], rwkv_ln_b[l])
        lambda_init = 0.8 - 0.6 * math.exp(-0.3 * l)
        y_d = _diff_attn(p_d, bias_df, diff_q_gain[l], diff_k_gain[l], diff_lambda[l],
                         diff_subln_g[l], lambda_init)
        y_m = _moba(p_m, bias_mb, moba_q_gain[l], moba_k_gain[l])
        w_router = jnp.zeros((D, LANES), F32).at[:, :n_route].set(
            jnp.concatenate([router_g_w[l], router_e_w[l]], axis=1))
        b_router = jnp.zeros((1, LANES), F32).at[0, :n_route].set(
            jnp.concatenate([router_g_b[l], router_e_b[l]]))
        x1, h2p, route = _outproj(y_r, y_d, y_m, x, g1, sc2, sh2, norm2_g[l], w_out[l].astype(BF16),
                                  w_router, b_router)
        moe_args = (h2p.reshape(T, D // 2), route.reshape(T, LANES), x1.reshape(T, D), g2,
                    moe_w1, moe_w3, moe_w2, S, l)
        if l + 1 < DEPTH:
            nxt = (norm1_g[l + 1], mod(l + 1, 1), mod(l + 1, 0), w_in[l + 1].astype(BF16))
            x2, p_r, p_d, p_m = _moe(*moe_args, next_ln=nxt)
            p_r, p_d, p_m = (p_r.reshape(B, S, R_COLS), p_d.reshape(B, S, DF_COLS),
                             p_m.reshape(B, S, MB_COLS))
        else:
            (x2,) = _moe(*moe_args)
        x = x2.reshape(B, S, D)
    return x
```

```python
import functools
import math

import jax
import jax.numpy as jnp
from jax import lax
from jax.experimental import pallas as pl
from jax.experimental.pallas import tpu as pltpu

F32 = jnp.float32
BF16 = jnp.bfloat16
HIGHEST = lax.Precision.HIGHEST

D_MODEL = 1024
DEPTH = 4
HEAD_DIM = 64
R_HEADS = 4
R_WIDTH = 256
R_LORA_W = 32
R_LORA_A = 32
R_LORA_G = 64
R_COLS = 896
R_GN_EPS = 64e-5
DF_HEADS = 4
DF_V = 128
DF_WIDTH = 512
DF_COLS = 1536
MB_HEADS = 4
MB_WIDTH = 256
MB_COLS = 768
MB_BLOCK = 256
MB_TOPK = 3
IN_COLS = 3200
NUM_BUCKETS = 32
MAX_DISTANCE = 1024
N_GROUPS = 4
EXPERTS_PER_GROUP = 8
N_EXPERTS = 32
TOP_K_INNER = 2
EXPERT_HIDDEN = 512
MOE_BLOCK = 256
RMS_EPS = 1e-6

LANES = 128
SUBLANES = 8
ATT_TILE = 256
SAT_TILE = 5
MASK_TILE = 6
N_BIAS_TILES = 7
ATT_UNROLL = 2
Q_HALVES = 2
RW_CHUNK = 64
RW_CHUNKS_PER_STEP = 8
NEG_BIG = -1e30
LOG2E = math.log2(math.e)
QK_SCALE = HEAD_DIM ** -0.5 * LOG2E
VMEM_LIMIT = 48 * 1024 * 1024


def _cparams(sem):
    return pltpu.CompilerParams(dimension_semantics=sem, vmem_limit_bytes=VMEM_LIMIT)


def _mm(a, b):
    return jnp.dot(a.astype(BF16), b.astype(BF16), preferred_element_type=F32)


def _mm_nt(a, b):
    return lax.dot_general(a.astype(BF16), b.astype(BF16), (((1,), (1,)), ((), ())),
                           preferred_element_type=F32)


def _mm_tn(a, b):
    return lax.dot_general(a.astype(BF16), b.astype(BF16), (((0,), (0,)), ((), ())),
                           preferred_element_type=F32)


def _mm_hi(a, b):
    return jnp.dot(a, b, precision=HIGHEST, preferred_element_type=F32)


def _mm_3pass(a, b):
    a_hi = a.astype(BF16)
    a_lo = (a - a_hi.astype(F32)).astype(BF16)
    b_hi = b.astype(BF16)
    b_lo = (b - b_hi.astype(F32)).astype(BF16)
    m = a.shape[0]
    first = jnp.dot(jnp.concatenate([a_hi, a_lo], axis=0), b_hi, preferred_element_type=F32)
    return first[:m] + first[m:] + jnp.dot(a_hi, b_lo, preferred_element_type=F32)


def _split_bf16(x, terms):
    parts = []
    for _ in range(terms):
        p = x.astype(BF16)
        parts.append(p)
        x = x - p.astype(F32)
    return parts


def _mm_exact_rhs(a, b01, terms=2):
    m = a.shape[0]
    out = jnp.dot(jnp.concatenate(_split_bf16(a, terms), axis=0), b01.astype(BF16), preferred_element_type=F32)
    return functools.reduce(lambda x, y: x + y, [out[t * m:(t + 1) * m] for t in range(terms)])


def _mm_exact_lhs(a01, b, terms=3):
    n = b.shape[1]
    out = jnp.dot(a01.astype(BF16), jnp.concatenate(_split_bf16(b, terms), axis=1), preferred_element_type=F32)
    return functools.reduce(lambda x, y: x + y, [out[:, t * n:(t + 1) * n] for t in range(terms)])


def _adaln_kernel(c_ref, w_ref, b_ref, o_ref):
    c = c_ref[...]
    cs = c * jax.nn.sigmoid(c)
    o_ref[0] = _mm_hi(cs, w_ref[0]) + b_ref[0]


def _adaln(c, ada_w, ada_b):
    L, D, D6 = ada_w.shape
    B = c.shape[0]
    nj = D6 // D
    return pl.pallas_call(
        _adaln_kernel,
        grid=(L, nj),
        in_specs=[pl.BlockSpec((B, D), lambda l, j: (0, 0)),
                  pl.BlockSpec((1, D, D), lambda l, j: (l, 0, j)),
                  pl.BlockSpec((1, 1, D), lambda l, j: (l, 0, j))],
        out_specs=pl.BlockSpec((1, B, D), lambda l, j: (l, 0, j)),
        out_shape=jax.ShapeDtypeStruct((L, B, D6), F32),
        compiler_params=_cparams(("parallel", "parallel")),
        name="adaln",
    )(c, ada_w, ada_b.reshape(L, 1, D6))


def _norm_qk_cols(p, n_qk, gains_ref, row):
    per = n_qk // LANES
    chunks = []
    for c in range(p.shape[1] // LANES):
        x = p[:, c * LANES:(c + 1) * LANES]
        if c < 2 * per:
            x = _rms_halves(x, gains_ref[row + c // per:row + c // per + 1, :])
        chunks.append(x)
    return jnp.concatenate(chunks, axis=1).astype(BF16)


def _project(h, w_ref, gains_ref):
    c1 = R_COLS
    c2 = R_COLS + DF_COLS
    p_r = jnp.dot(h, w_ref[:, :c1], preferred_element_type=F32)
    p_d = _norm_qk_cols(jnp.dot(h, w_ref[:, c1:c2], preferred_element_type=F32), DF_WIDTH, gains_ref, 0)
    p_m = _norm_qk_cols(jnp.dot(h, w_ref[:, c2:], preferred_element_type=F32), MB_WIDTH, gains_ref, 2)
    return p_r, p_d, p_m


def _ln_inproj_kernel(x_ref, g_ref, sc_ref, sh_ref, w_ref, gains_ref, pr_ref, pd_ref, pm_ref):
    x = x_ref[0]
    y = x * lax.rsqrt(jnp.mean(x * x, -1, keepdims=True) + RMS_EPS)
    h = ((y * g_ref[...]) * (1.0 + sc_ref[0]) + sh_ref[0]).astype(BF16)
    pr_ref[0], pd_ref[0], pm_ref[0] = _project(h, w_ref, gains_ref)


def _ln_inproj(x, g, sc, sh, w_bf16, gains, tm=512):
    B, S, D = x.shape
    return pl.pallas_call(
        _ln_inproj_kernel,
        grid=(B, S // tm),
        in_specs=[pl.BlockSpec((1, tm, D), lambda b, i: (b, i, 0)),
                  pl.BlockSpec((1, D), lambda b, i: (0, 0)),
                  pl.BlockSpec((1, 1, D), lambda b, i: (b, 0, 0)),
                  pl.BlockSpec((1, 1, D), lambda b, i: (b, 0, 0)),
                  pl.BlockSpec((D, IN_COLS), lambda b, i: (0, 0)),
                  pl.BlockSpec(gains.shape, lambda b, i: (0, 0))],
        out_specs=[pl.BlockSpec((1, tm, R_COLS), lambda b, i: (b, i, 0)),
                   pl.BlockSpec((1, tm, DF_COLS), lambda b, i: (b, i, 0)),
                   pl.BlockSpec((1, tm, MB_COLS), lambda b, i: (b, i, 0))],
        out_shape=[jax.ShapeDtypeStruct((B, S, R_COLS), F32),
                   jax.ShapeDtypeStruct((B, S, DF_COLS), BF16),
                   jax.ShapeDtypeStruct((B, S, MB_COLS), BF16)],
        compiler_params=_cparams(("parallel", "parallel")),
        name="ln_inproj",
    )(x, g.reshape(1, D), sc.reshape(B, 1, D), sh.reshape(B, 1, D), w_bf16, gains)


def _head_ones(n):
    r = lax.broadcasted_iota(jnp.int32, (n, n), 0) // HEAD_DIM
    c = lax.broadcasted_iota(jnp.int32, (n, n), 1) // HEAD_DIM
    return (r == c).astype(F32)


def _rwkv_prep_kernel(p_ref, pp_ref, mu_ref, w0_ref, w2_ref, a0_ref, a2_ref, g2_ref, kk_ref, ka_ref,
                      rk_ref, r_o, k_o, v_o, kap_o, b_o, lw_o, g_o, bonus_o):
    i = pl.program_id(1)
    p = p_ref[0]
    prev_row = jnp.where(i > 0, pp_ref[0][7:8, :], 0.0)
    rows = lax.broadcasted_iota(jnp.int32, p.shape, 0)
    p_prev = jnp.where(rows == 0, prev_row, pltpu.roll(p, 1, 0))
    ps = p + (p_prev - p) * mu_ref[...]
    W = R_WIDTH
    r = ps[:, 0:W]
    k = ps[:, W:2 * W]
    v = ps[:, 2 * W:3 * W]
    c3 = 3 * W
    c4 = c3 + R_LORA_W
    c5 = c4 + R_LORA_A
    wd = ps[:, c3:c4]
    ad = ps[:, c4:c5]
    gd = ps[:, c5:R_COLS]
    z = -(w0_ref[...] + _mm_3pass(jnp.tanh(wd), w2_ref[...]))
    softplus = jnp.maximum(z, 0.0) + jnp.log(1.0 + jnp.exp(-jnp.abs(z)))
    lw_o[0] = -jnp.exp(-softplus - 0.5)
    a = jax.nn.sigmoid(a0_ref[...] + _mm_3pass(ad, a2_ref[...]))
    g_o[0] = _mm_3pass(jax.nn.sigmoid(gd), g2_ref[...])
    ones = _head_ones(W)
    kk = k * kk_ref[...]
    nrm = jnp.sqrt(_mm_exact_rhs(kk * kk, ones))
    kap = kk / jnp.maximum(nrm, 1e-12)
    k2 = k * (1.0 + (a - 1.0) * ka_ref[...])
    r_o[0] = r
    k_o[0] = k2
    v_o[0] = v
    kap_o[0] = kap
    b_o[0] = kap * a
    bonus_o[0] = _mm_exact_rhs(r * k2 * rk_ref[...], ones) * v


def _rwkv_prep(p_r, mu, w0, w2, a0, a2, g2, k_k, k_a, r_k, tm=512):
    B, S, _ = p_r.shape
    W = R_WIDTH
    row = lambda a: a.reshape(1, -1)
    full = lambda a: pl.BlockSpec(a.shape, lambda b, i: (0,) * a.ndim)
    ins = [row(mu), row(w0), w2, row(a0), a2, g2, row(k_k), row(k_a), row(r_k)]
    out_spec = pl.BlockSpec((1, tm, W), lambda b, i: (b, i, 0))
    return pl.pallas_call(
        _rwkv_prep_kernel,
        grid=(B, S // tm),
        in_specs=[pl.BlockSpec((1, tm, R_COLS), lambda b, i: (b, i, 0)),
                  pl.BlockSpec((1, 8, R_COLS), lambda b, i: (b, jnp.maximum(i * (tm // 8) - 1, 0), 0))]
                 + [full(a) for a in ins],
        out_specs=[out_spec] * 8,
        out_shape=[jax.ShapeDtypeStruct((B, S, W), F32)] * 8,
        compiler_params=_cparams(("parallel", "arbitrary")),
        name="rwkv_prep",
    )(p_r, p_r, *ins)


def _rwkv_chunk_kernel(r_ref, k_ref, v_ref, kap_ref, b_ref, lw_ref, g_ref, bonus_ref, lng_ref, lnb_ref,
                       o_ref, state):
    step = pl.program_id(1)
    C = RW_CHUNK
    NC = RW_CHUNKS_PER_STEP
    Dh = HEAD_DIM
    TM = NC * C

    @pl.when(step == 0)
    def _():
        state[...] = jnp.zeros_like(state)

    ti = lax.broadcasted_iota(jnp.int32, (C, C), 0)
    tj = lax.broadcasted_iota(jnp.int32, (C, C), 1)
    eye = (ti == tj).astype(F32)
    ti2 = lax.broadcasted_iota(jnp.int32, (C, 2 * C), 0)
    tj2 = lax.broadcasted_iota(jnp.int32, (C, 2 * C), 1) % C
    strict2 = tj2 < ti2
    incl2 = tj2 <= ti2

    bi = lax.broadcasted_iota(jnp.int32, (TM, TM), 0)
    bj = lax.broadcasted_iota(jnp.int32, (TM, TM), 1)
    chunk_tri = jnp.logical_and(bi // C == bj // C, bj <= bi).astype(F32)

    lw = lw_ref[0]
    cum = _mm_exact_lhs(chunk_tri, lw)
    cum_end = jnp.concatenate(
        [jnp.broadcast_to(cum[(c + 1) * C - 1:(c + 1) * C, :], (C, R_WIDTH)) for c in range(NC)], axis=0)
    e_pos = jnp.exp(cum)
    e_neg = jnp.exp(-cum)
    e_prev = jnp.exp(cum - lw)
    e_end = jnp.exp(cum_end - cum)
    g_end = jnp.exp(cum_end)
    r = r_ref[0]
    k = k_ref[0]
    v = v_ref[0]
    b = b_ref[0]
    rt = r * e_pos
    kapt = kap_ref[0] * e_prev
    bt = b * e_neg
    kt = k * e_neg
    bh = b * e_end
    kh = k * e_end

    items = [(c, h) for c in range(NC) for h in range(R_HEADS)]

    def tile(x, c, h):
        return x[c * C:(c + 1) * C, h * Dh:(h + 1) * Dh]

    a_top, a_bot = [], []
    for c, h in items:
        lhs = jnp.concatenate([tile(kapt, c, h), tile(rt, c, h)], axis=0)
        rhs = jnp.concatenate([tile(bt, c, h), tile(kt, c, h)], axis=0)
        a = _mm_nt(lhs, rhs)
        a_top.append(jnp.where(strict2, a[:C], 0.0))
        a_bot.append(jnp.where(incl2, a[C:], 0.0))
    xp = [-t[:, :C] for t in a_top]
    tinv = [eye + x for x in xp]
    xp = [_mm(x, x) for x in xp]
    for _ in range(4):
        prod = [_mm(jnp.concatenate([t, x], axis=0), x) for t, x in zip(tinv, xp)]
        tinv = [t + p[:C] for t, p in zip(tinv, prod)]
        xp = [p[C:] for p in prod]
    tinv = [t + _mm(t, x) for t, x in zip(tinv, xp)]
    vt = [tile(v, c, h) for c, h in items]
    lkv = [_mm(t[:, C:], vv) for t, vv in zip(a_top, vt)]
    w = [_mm(t, tile(kapt, c, h)) for t, (c, h) in zip(tinv, items)]
    ploc = [-_mm(t, x) for t, x in zip(tinv, lkv)]
    q_eff = [tile(rt, c, h) - _mm(ab[:, :C], ww) for ab, ww, (c, h) in zip(a_bot, w, items)]
    pv = [jnp.concatenate([p, vv], axis=0) for p, vv in zip(ploc, vt)]
    y_loc = [_mm(ab, x) for ab, x in zip(a_bot, pv)]
    g_mat = [eye * tile(g_end, c, h)[0:1, :] - _mm_tn(ww, tile(bh, c, h)) for ww, (c, h) in zip(w, items)]
    u_mat = [_mm_tn(x, jnp.concatenate([tile(bh, c, h), tile(kh, c, h)], axis=0))
             for x, (c, h) in zip(pv, items)]

    s_cur = [state[h] for h in range(R_HEADS)]
    ys = [[None] * NC for _ in range(R_HEADS)]
    for n, (c, h) in enumerate(items):
        ys[h][c] = _mm_nt(q_eff[n], s_cur[h]) + y_loc[n]
        s_cur[h] = _mm(s_cur[h], g_mat[n]) + u_mat[n]
    for h in range(R_HEADS):
        state[h] = s_cur[h]

    gate = g_ref[0]
    bonus = bonus_ref[0]
    lng = lng_ref[...]
    lnb = lnb_ref[...]
    lane = lax.broadcasted_iota(jnp.int32, (Dh, R_WIDTH), 1)
    sub = lax.broadcasted_iota(jnp.int32, (Dh, R_WIDTH), 0)
    out = jnp.zeros((TM, R_WIDTH), F32)
    for h in range(R_HEADS):
        sl = slice(h * Dh, (h + 1) * Dh)
        y = jnp.concatenate(ys[h], axis=0)
        mean = jnp.mean(y, -1, keepdims=True)
        yc = y - mean
        var = jnp.mean(yc * yc, -1, keepdims=True)
        yn = yc * lax.rsqrt(var + R_GN_EPS)
        yfin = (yn * lng[:, sl] + lnb[:, sl] + bonus[:, sl]) * gate[:, sl]
        place = (lane == sub + h * Dh).astype(BF16)
        out = out + jnp.dot(yfin.astype(BF16), place, preferred_element_type=F32)
    o_ref[0] = out.astype(BF16)


def _rwkv_chunk(r, k, v, kap, b, lw, g, bonus, ln_g, ln_b):
    B, S, W = r.shape
    tm = RW_CHUNK * RW_CHUNKS_PER_STEP
    spec = pl.BlockSpec((1, tm, W), lambda bb, c: (bb, c, 0))
    vec = pl.BlockSpec((1, W), lambda bb, c: (0, 0))
    return pl.pallas_call(
        _rwkv_chunk_kernel,
        grid=(B, S // tm),
        in_specs=[spec] * 8 + [vec, vec],
        out_specs=spec,
        out_shape=jax.ShapeDtypeStruct((B, S, W), BF16),
        scratch_shapes=[pltpu.VMEM((R_HEADS, HEAD_DIM, HEAD_DIM), F32)],
        compiler_params=_cparams(("parallel", "arbitrary")),
        name="rwkv_chunk",
    )(r, k, v, kap, b, lw, g, bonus, ln_g.reshape(1, W), ln_b.reshape(1, W))


def _t5_bucket(dist):
    n = jnp.maximum(dist, 0)
    max_exact = NUM_BUCKETS // 2
    nf = jnp.maximum(n, 1).astype(F32)
    large = max_exact + (jnp.log(nf / max_exact) / math.log(MAX_DISTANCE / max_exact)
                         * (NUM_BUCKETS - max_exact)).astype(jnp.int32)
    large = jnp.minimum(large, NUM_BUCKETS - 1)
    return jnp.where(n < max_exact, n, large)


def _bias_tiles(tbl):
    T = ATT_TILE
    assert (SAT_TILE - 1) * T + 1 >= MAX_DISTANCE
    nh = tbl.shape[1]
    o = jnp.arange(SAT_TILE + 1)[:, None, None]
    i = jnp.arange(T)[None, :, None]
    j = jnp.arange(T)[None, None, :]
    dist = o * T + i - j
    onehot = (_t5_bucket(dist)[None] == jnp.arange(NUM_BUCKETS)[:, None, None, None]).astype(F32)
    tiles = jnp.dot(tbl.astype(F32).T, onehot.reshape(NUM_BUCKETS, -1), precision=HIGHEST)
    tiles = jnp.where((dist >= 0)[None], tiles.reshape(nh, SAT_TILE + 1, T, T) * LOG2E, NEG_BIG)
    return jnp.concatenate([tiles, jnp.full((nh, 1, T, T), NEG_BIG, F32)], axis=1)


def _half_masks(shape):
    lane = lax.broadcasted_iota(jnp.int32, shape, len(shape) - 1)
    lo = lane < HEAD_DIM
    return lo, jnp.logical_not(lo)


def _rms_halves(x, gain2):
    lo, hi = _half_masks(x.shape)
    sq = x * x
    s_lo = jnp.sum(jnp.where(lo, sq, 0.0), -1, keepdims=True)
    s_hi = jnp.sum(jnp.where(hi, sq, 0.0), -1, keepdims=True)
    inv = jnp.where(lo, lax.rsqrt(s_lo / HEAD_DIM + RMS_EPS), lax.rsqrt(s_hi / HEAD_DIM + RMS_EPS))
    return x * inv * gain2


def _lane_chunks(x):
    return [x[:, c * LANES:(c + 1) * LANES] for c in range(x.shape[1] // LANES)]


def _flash(q2, kn, v_ref, bias_tile, qi, s_bufs, m_s, l_s, acc_s):
    T = ATT_TILE
    U = ATT_UNROLL
    H = Q_HALVES
    n_iter = (qi * H + H - 1) // U + 1

    def window(i):
        return pl.ds(pl.multiple_of(jnp.minimum(i, n_iter - 1) * (U * T), U * T), U * T)

    def produce(i, dst):
        s = lax.dot_general(q2, kn[window(i), :], (((1,), (1,)), ((), ())), preferred_element_type=F32)
        idx = {}
        for d in range(-(U - 1), H):
            off = qi * H - i * U + d
            idx[d] = jnp.where(off < 0, MASK_TILE, jnp.minimum(off, SAT_TILE))
        bias = jnp.concatenate(
            [jnp.concatenate([bias_tile(g, idx[a - u]) for u in range(U)], axis=1)
             for g in range(2) for a in range(H)], axis=0)
        dst[...] = s + bias

    def consume(i, src):
        parts = _lane_chunks(src[...])
        m_old = m_s[...]
        blk_max = jnp.max(functools.reduce(jnp.maximum, parts), -1, keepdims=True)
        m_new = jnp.maximum(m_old, jnp.broadcast_to(blk_max, m_old.shape))
        alpha = jnp.exp2(m_old - m_new)
        ps = [jnp.exp2(part - m_new) for part in parts]
        l_s[...] = alpha * l_s[...] + functools.reduce(lambda a, b: a + b, ps)
        p = jnp.concatenate(ps, axis=1).astype(BF16)
        acc_s[...] = alpha * acc_s[...] + jnp.dot(p, v_ref[0, window(i), :], preferred_element_type=F32)
        m_s[...] = m_new

    m_s[...] = jnp.full_like(m_s, -jnp.inf)
    l_s[...] = jnp.zeros_like(l_s)
    acc_s[...] = jnp.zeros_like(acc_s)
    produce(0, s_bufs[0])

    def body(i, carry):
        for par in range(2):
            @pl.when(i % 2 == par)
            def _():
                produce(i + 1, s_bufs[1 - par])
                consume(i, s_bufs[par])
        return carry

    lax.fori_loop(0, n_iter, body, 0)
    return acc_s[...] / jnp.sum(l_s[...], -1, keepdims=True)


def _diff_attn_kernel(q_ref, k_ref, v_ref, bias_ref, lam_ref, sg_ref, o_ref,
                      s_a, s_b, m_s, l_s, acc_s, *, lambda_init):
    qi = pl.program_id(2)
    q = q_ref[0]
    tq = q.shape[0]
    lo, hi = _half_masks(q.shape)
    zero = jnp.zeros_like(q)
    q2 = jnp.concatenate([jnp.where(lo, q, zero), jnp.where(hi, q, zero)], axis=0)

    def bias_tile(g, idx):
        return bias_ref[0, idx]

    o = _flash(q2, k_ref.at[0], v_ref, bias_tile, qi, (s_a, s_b), m_s, l_s, acc_s)
    lam = lam_ref[...]
    lam_full = (jnp.exp(jnp.sum(lam[0:1] * lam[1:2], -1, keepdims=True))
                - jnp.exp(jnp.sum(lam[2:3] * lam[3:4], -1, keepdims=True)) + lambda_init)
    out = o[:tq] - lam_full * o[tq:]
    out = out * lax.rsqrt(jnp.mean(out * out, -1, keepdims=True) + RMS_EPS) * sg_ref[...]
    o_ref[0] = (out * (1.0 - lambda_init)).astype(BF16)


def _diff_attn(p_d, bias_tiles, lam, subln_g, lambda_init):
    B, S, _ = p_d.shape
    T = ATT_TILE
    H = DF_HEADS
    TQ = Q_HALVES * T
    R = 2 * TQ
    assert S % (ATT_UNROLL * T) == 0 and S % TQ == 0
    kern = functools.partial(_diff_attn_kernel, lambda_init=lambda_init)
    return pl.pallas_call(
        kern,
        grid=(B, H, S // TQ),
        in_specs=[pl.BlockSpec((1, TQ, LANES), lambda b, h, i: (b, i, h)),
                  pl.BlockSpec((1, S, LANES), lambda b, h, i: (b, 0, H + h)),
                  pl.BlockSpec((1, S, LANES), lambda b, h, i: (b, 0, 2 * H + h)),
                  pl.BlockSpec((1, N_BIAS_TILES, T, T), lambda b, h, i: (h, 0, 0, 0)),
                  pl.BlockSpec((4, HEAD_DIM), lambda b, h, i: (0, 0)),
                  pl.BlockSpec((1, DF_V), lambda b, h, i: (0, 0))],
        out_specs=pl.BlockSpec((1, TQ, DF_V), lambda b, h, i: (b, i, h)),
        out_shape=jax.ShapeDtypeStruct((B, S, DF_WIDTH), BF16),
        scratch_shapes=[pltpu.VMEM((R, ATT_UNROLL * T), F32),
                        pltpu.VMEM((R, ATT_UNROLL * T), F32),
                        pltpu.VMEM((R, LANES), F32),
                        pltpu.VMEM((R, LANES), F32),
                        pltpu.VMEM((R, DF_V), F32)],
        compiler_params=_cparams(("parallel", "parallel", "arbitrary")),
        name="diff_attn",
    )(p_d, p_d, p_d, bias_tiles, lam, subln_g.reshape(1, DF_V))


def _moba_kernel(q_ref, k_ref, v_ref, bias_ref, o_ref, kn, kmean, s_a, s_b, m_s, l_s, acc_s):
    qi = pl.program_id(2)
    T = ATT_TILE
    nkb = k_ref.shape[1] // T

    @pl.when(qi == 0)
    def _():
        kmean[...] = jnp.zeros_like(kmean)
        lane = lax.broadcasted_iota(jnp.int32, (T, LANES), 1)

        def body(j, carry):
            rows = pl.ds(pl.multiple_of(j * T, T), T)
            kb = k_ref[0, rows, :]
            kn[rows, :] = jnp.concatenate([kb, (lane == j).astype(BF16)], axis=1)
            kmean[pl.ds(j, 1), :] = jnp.mean(kb.astype(F32), 0, keepdims=True)
            return carry
        lax.fori_loop(0, nkb, body, 0)

    q = q_ref[0]
    tq = q.shape[0]
    lo, hi = _half_masks(q.shape)
    zero = jnp.zeros_like(q)
    q2b = jnp.concatenate([jnp.where(lo, q, zero), jnp.where(hi, q, zero)], axis=0)

    nkp = kmean.shape[0]
    g2 = lax.dot_general(jnp.concatenate(_split_bf16(kmean[...], 2), axis=0), q2b, (((1,), (1,)), ((), ())),
                         preferred_element_type=F32)
    gate = g2[:nkp] + g2[nkp:]
    blk = lax.broadcasted_iota(jnp.int32, gate.shape, 0).astype(F32)
    col = lax.broadcasted_iota(jnp.int32, gate.shape, 1)
    own = (qi * Q_HALVES + (col % tq) // T).astype(F32)
    past = blk < own
    gate = jnp.where(past, gate, -jnp.inf)
    pen_t = jnp.where(past, NEG_BIG, 0.0)
    for _ in range(MB_TOPK):
        mx = jnp.max(gate, 0, keepdims=True)
        first = jnp.min(jnp.where(gate == mx, blk, float(nkp)), 0, keepdims=True)
        pick = jnp.logical_and(blk == first, mx > -jnp.inf)
        pen_t = jnp.where(pick, 0.0, pen_t)
        gate = jnp.where(pick, -jnp.inf, gate)
    pen = jnp.concatenate([pen_t, jnp.zeros((LANES - nkp, 2 * tq), F32)], axis=0).T
    q2 = jnp.concatenate([q2b, pen.astype(BF16)], axis=1)

    def bias_tile(g, idx):
        return bias_ref[g, idx]

    o = _flash(q2, kn, v_ref, bias_tile, qi, (s_a, s_b), m_s, l_s, acc_s)
    lo_o, _ = _half_masks((tq, LANES))
    o_ref[0] = jnp.where(lo_o, o[:tq], o[tq:]).astype(BF16)


def _moba(p_m, bias_tiles):
    B, S, _ = p_m.shape
    T = ATT_TILE
    HP = MB_HEADS // 2
    TQ = Q_HALVES * T
    R = 2 * TQ
    assert T == MB_BLOCK and S // T <= LANES and S % (ATT_UNROLL * T) == 0 and S % TQ == 0
    return pl.pallas_call(
        _moba_kernel,
        grid=(B, HP, S // TQ),
        in_specs=[pl.BlockSpec((1, TQ, LANES), lambda b, h, i: (b, i, h)),
                  pl.BlockSpec((1, S, LANES), lambda b, h, i: (b, 0, HP + h)),
                  pl.BlockSpec((1, S, LANES), lambda b, h, i: (b, 0, 2 * HP + h)),
                  pl.BlockSpec((2, N_BIAS_TILES, T, T), lambda b, h, i: (h, 0, 0, 0))],
        out_specs=pl.BlockSpec((1, TQ, LANES), lambda b, h, i: (b, i, h)),
        out_shape=jax.ShapeDtypeStruct((B, S, MB_WIDTH), BF16),
        scratch_shapes=[pltpu.VMEM((S, 2 * LANES), BF16),
                        pltpu.VMEM((-(-(S // T) // 8) * 8, LANES), F32),
                        pltpu.VMEM((R, ATT_UNROLL * T), F32),
                        pltpu.VMEM((R, ATT_UNROLL * T), F32),
                        pltpu.VMEM((R, LANES), F32),
                        pltpu.VMEM((R, LANES), F32),
                        pltpu.VMEM((R, LANES), F32)],
        compiler_params=_cparams(("parallel", "parallel", "arbitrary")),
        name="moba",
    )(p_m, p_m, p_m, bias_tiles)


def _pack_pairs(x):
    n = x.shape[1] // 2
    hi = pltpu.bitcast(x[:, :n].astype(BF16).astype(F32), jnp.uint32)
    lo = pltpu.bitcast(x[:, n:].astype(BF16).astype(F32), jnp.uint32)
    return hi | (lo >> 16)


def _unpack_pairs(p):
    hi = pltpu.bitcast(p & jnp.uint32(0xFFFF0000), F32)
    lo = pltpu.bitcast(p << 16, F32)
    return jnp.concatenate([hi, lo], axis=1)


def _outproj_kernel(yr_ref, yd_ref, ym_ref, x_ref, g1_ref, sc_ref, sh_ref, ng_ref, w_ref, wr_ref, br_ref,
                    x1_ref, h2_ref, route_ref):
    c1 = R_WIDTH
    c2 = R_WIDTH + DF_WIDTH
    mix = (jnp.dot(yr_ref[0], w_ref[:c1, :], preferred_element_type=F32)
           + jnp.dot(yd_ref[0], w_ref[c1:c2, :], preferred_element_type=F32)
           + jnp.dot(ym_ref[0], w_ref[c2:, :], preferred_element_type=F32))
    x1 = x_ref[0] + g1_ref[0] * mix
    x1_ref[0] = x1
    y = x1 * lax.rsqrt(jnp.mean(x1 * x1, -1, keepdims=True) + RMS_EPS)
    h2 = (y * ng_ref[...]) * (1.0 + sc_ref[0]) + sh_ref[0]
    h2_ref[0] = _pack_pairs(h2)

    logits = _mm_3pass(h2, wr_ref[...]) + br_ref[...]
    lane = lax.broadcasted_iota(jnp.int32, logits.shape, 1).astype(F32)
    far = float(LANES)
    is_g = lane < N_GROUPS
    gl = jnp.where(is_g, logits, -jnp.inf)
    gmax = jnp.max(gl, -1, keepdims=True)
    g_idx = jnp.min(jnp.where(gl == gmax, lane, far), -1, keepdims=True)
    pg_top = 1.0 / jnp.sum(jnp.where(is_g, jnp.exp(logits - gmax), 0.0), -1, keepdims=True)
    e_lo = N_GROUPS + EXPERTS_PER_GROUP * g_idx
    in_grp = jnp.logical_and(lane >= e_lo, lane < e_lo + EXPERTS_PER_GROUP)
    el = jnp.where(in_grp, logits, -jnp.inf)
    m1 = jnp.max(el, -1, keepdims=True)
    i1 = jnp.min(jnp.where(el == m1, lane, far), -1, keepdims=True)
    el2 = jnp.where(lane == i1, -jnp.inf, el)
    m2 = jnp.max(el2, -1, keepdims=True)
    i2 = jnp.min(jnp.where(el2 == m2, lane, far), -1, keepdims=True)
    e2 = jnp.exp(m2 - m1)
    gate1 = pg_top / (1.0 + e2)
    gate2 = gate1 * e2
    route = jnp.where(lane == 0, gate1, 0.0)
    route = jnp.where(lane == 1, gate2, route)
    route = jnp.where(lane == 2, i1 - N_GROUPS, route)
    route = jnp.where(lane == 3, i2 - N_GROUPS, route)
    route_ref[0] = route


def _outproj(y_r, y_d, y_m, x, g1, sc2, sh2, norm2_g, w_out_bf16, w_router, b_router, tm=512):
    B, S, D = x.shape
    vec = pl.BlockSpec((1, 1, D), lambda b, i: (b, 0, 0))
    tok = lambda w: pl.BlockSpec((1, tm, w), lambda b, i: (b, i, 0))
    full = lambda a: pl.BlockSpec(a.shape, lambda b, i: (0,) * a.ndim)
    return pl.pallas_call(
        _outproj_kernel,
        grid=(B, S // tm),
        in_specs=[tok(R_WIDTH), tok(DF_WIDTH), tok(MB_WIDTH), tok(D), vec, vec, vec,
                  pl.BlockSpec((1, D), lambda b, i: (0, 0)), full(w_out_bf16), full(w_router), full(b_router)],
        out_specs=[tok(D), tok(D // 2), tok(LANES)],
        out_shape=[jax.ShapeDtypeStruct((B, S, D), F32),
                   jax.ShapeDtypeStruct((B, S, D // 2), jnp.uint32),
                   jax.ShapeDtypeStruct((B, S, LANES), F32)],
        compiler_params=_cparams(("parallel", "parallel")),
        name="outproj_router",
    )(y_r, y_d, y_m, x, g1.reshape(B, 1, D), sc2.reshape(B, 1, D), sh2.reshape(B, 1, D),
      norm2_g.reshape(1, D), w_out_bf16, w_router, b_router)


def _rank_kernel(route_ref, rank_ref, cnt_ref, carry):
    i = pl.program_id(0)

    @pl.when(i == 0)
    def _():
        carry[...] = jnp.zeros_like(carry)

    rt = route_ref[...]
    tm = rt.shape[0]
    lane = lax.broadcasted_iota(jnp.int32, rt.shape, 1).astype(F32)
    oh0 = (lane == rt[:, 2:3]).astype(F32)
    oh1 = (lane == rt[:, 3:4]).astype(F32)
    oh = oh0 + oh1
    row = lax.broadcasted_iota(jnp.int32, (tm, tm), 0)
    col = lax.broadcasted_iota(jnp.int32, (tm, tm), 1)
    before = jnp.dot((col < row).astype(BF16), oh.astype(BF16), preferred_element_type=F32) + carry[...]
    r0 = jnp.sum(oh0 * before, -1, keepdims=True)
    r1 = jnp.sum(oh1 * before, -1, keepdims=True)
    rank_ref[...] = jnp.where(lane == 0, r0, jnp.where(lane == 1, r1, 0.0))
    carry[...] = carry[...] + jnp.sum(oh, 0, keepdims=True)
    cnt_ref[...] = carry[...]


def _ranks(route, tm=512):
    T = route.shape[0]
    return pl.pallas_call(
        _rank_kernel,
        grid=(T // tm,),
        in_specs=[pl.BlockSpec((tm, LANES), lambda i: (i, 0))],
        out_specs=[pl.BlockSpec((tm, LANES), lambda i: (i, 0)), pl.BlockSpec((1, LANES), lambda i: (0, 0))],
        out_shape=[jax.ShapeDtypeStruct((T, LANES), F32), jax.ShapeDtypeStruct((1, LANES), F32)],
        scratch_shapes=[pltpu.VMEM((1, LANES), F32)],
        compiler_params=_cparams(("arbitrary",)),
        name="moe_ranks",
    )(route)


def _dispatch_kernel(hi_ref, lo_ref, h_ref, buf_in, buf_out, sem):
    del buf_in
    groups = h_ref.shape[0]

    def row_copy(g, r, hi, lo):
        return pltpu.make_async_copy(h_ref.at[g, pl.ds(r, 1)], buf_out.at[hi, pl.ds(lo, 1)], sem)

    def issue(g, carry):
        for r in range(SUBLANES):
            for k in range(TOP_K_INNER):
                a = TOP_K_INNER * (g * SUBLANES + r) + k
                row_copy(g, r, hi_ref[a], lo_ref[a]).start(priority=k % 2)
        return carry

    def drain(g, carry):
        for _ in range(SUBLANES * TOP_K_INNER):
            row_copy(0, 0, 0, 0).wait()
        return carry

    lax.fori_loop(0, groups, issue, 0)
    lax.fori_loop(0, groups, drain, 0)


def _dispatch(h2p, dest_flat, n_slots, tm=512):
    T, W = h2p.shape
    S8 = SUBLANES
    buf0 = jnp.zeros((n_slots // S8, S8, W), jnp.uint32)
    smem = pl.BlockSpec((TOP_K_INNER * tm,), lambda i: (i,), memory_space=pltpu.SMEM)
    buf = pl.pallas_call(
        _dispatch_kernel,
        grid=(T // tm,),
        in_specs=[smem, smem,
                  pl.BlockSpec((tm // S8, S8, W), lambda i: (i, 0, 0)),
                  pl.BlockSpec(memory_space=pl.ANY)],
        out_specs=pl.BlockSpec(memory_space=pl.ANY),
        out_shape=jax.ShapeDtypeStruct((n_slots // S8, S8, W), jnp.uint32),
        scratch_shapes=[pltpu.SemaphoreType.DMA],
        input_output_aliases={3: 0},
        compiler_params=_cparams(("arbitrary",)),
        name="moe_dispatch",
    )(dest_flat // S8, dest_flat % S8, h2p.reshape(T // S8, S8, W), buf0)
    return buf.reshape(n_slots, W)


def _expert_kernel(be_ref, slot_ref, nxt_ref, nu_ref, x_ref, w1_hbm, w3_hbm, w2_hbm, o_ref,
                   w1_s, w3_s, w2_s, sems, *, layer):
    i = pl.program_id(0)
    used = i < nu_ref[0]
    e = be_ref[i]
    slot = slot_ref[i]
    run_start = jnp.logical_or(i == 0, e != be_ref[jnp.maximum(i - 1, 0)])

    def fetch(expert, s):
        return [pltpu.make_async_copy(w1_hbm.at[layer, expert], w1_s.at[s], sems.at[s, 0]),
                pltpu.make_async_copy(w3_hbm.at[layer, expert], w3_s.at[s], sems.at[s, 1]),
                pltpu.make_async_copy(w2_hbm.at[layer, expert], w2_s.at[s], sems.at[s, 2])]

    @pl.when(jnp.logical_and(used, run_start))
    def _():
        @pl.when(i == 0)
        def _():
            for cp in fetch(e, slot):
                cp.start()
        for cp in fetch(e, slot):
            cp.wait()
        nxt = nxt_ref[i]

        @pl.when(nxt >= 0)
        def _():
            for cp in fetch(nxt, 1 - slot):
                cp.start()

    @pl.when(used)
    def _():
        x = _unpack_pairs(x_ref[...]).astype(BF16)
        h1 = jnp.dot(x, w1_s[slot].astype(BF16), preferred_element_type=F32)
        h3 = jnp.dot(x, w3_s[slot].astype(BF16), preferred_element_type=F32)
        h = (h1 * jax.nn.sigmoid(h1) * h3).astype(BF16)
        o_ref[...] = _pack_pairs(jnp.dot(h, w2_s[slot].astype(BF16), preferred_element_type=F32))

    @pl.when(jnp.logical_not(used))
    def _():
        o_ref[...] = jnp.zeros_like(o_ref)


def _experts(buf, blk_e, counts, n_used, w1, w3, w2, layer):
    n_slots, W = buf.shape
    _, E, D, Hd = w1.shape
    nb = n_slots // MOE_BLOCK
    change = jnp.concatenate([jnp.ones((1,), jnp.int32), (blk_e[1:] != blk_e[:-1]).astype(jnp.int32)])
    slot = (jnp.cumsum(change) - 1) % 2
    ids = jnp.arange(E, dtype=jnp.int32)
    later = jnp.logical_and(ids[None, :] > ids[:, None], counts[None, :] > 0)
    nxt_e = jnp.min(jnp.where(later, ids[None, :], E), axis=1)
    nxt_e = jnp.where(nxt_e == E, -1, nxt_e)
    nxt = jnp.sum(jnp.where(blk_e[:, None] == ids[None, :], nxt_e[None, :], 0), axis=1)
    any_spec = pl.BlockSpec(memory_space=pl.ANY)
    grid_spec = pltpu.PrefetchScalarGridSpec(
        num_scalar_prefetch=4,
        grid=(nb,),
        in_specs=[pl.BlockSpec((MOE_BLOCK, W), lambda i, *_: (i, 0)), any_spec, any_spec, any_spec],
        out_specs=pl.BlockSpec((MOE_BLOCK, W), lambda i, *_: (i, 0)),
        scratch_shapes=[pltpu.VMEM((2, D, Hd), F32), pltpu.VMEM((2, D, Hd), F32),
                        pltpu.VMEM((2, Hd, D), F32), pltpu.SemaphoreType.DMA((2, 3))],
    )
    return pl.pallas_call(
        functools.partial(_expert_kernel, layer=layer),
        grid_spec=grid_spec,
        out_shape=jax.ShapeDtypeStruct((n_slots, W), jnp.uint32),
        compiler_params=_cparams(("arbitrary",)),
        name="moe_experts",
    )(blk_e, slot.astype(jnp.int32), nxt.astype(jnp.int32), n_used, buf, w1, w3, w2)


def _combine_kernel(hi_ref, lo_ref, hi_nxt_ref, lo_nxt_ref, yb_ref, route_ref, x_ref, g2_ref, *rest,
                    next_ln):
    if next_ln:
        ng_ref, sc_ref, sh_ref, w_ref, gains_ref, o_ref, pr_ref, pd_ref, pm_ref, rows, sems = rest
    else:
        o_ref, rows, sems = rest
    i = pl.program_id(0)
    n = pl.num_programs(0)
    tm = x_ref.shape[0]
    groups = tm // SUBLANES
    cur = i % 2

    def row_copy(buf, g, r, k, hi, lo):
        return pltpu.make_async_copy(yb_ref.at[hi, pl.ds(lo, 1)], rows.at[buf, k, g, pl.ds(r, 1)], sems.at[buf])

    def start_group(h_ref, l_ref, buf, g):
        for r in range(SUBLANES):
            for k in range(TOP_K_INNER):
                a = TOP_K_INNER * (g * SUBLANES + r) + k
                row_copy(buf, g, r, k, h_ref[a], l_ref[a]).start(priority=k % 2)

    def start_tile(h_ref, l_ref, buf):
        lax.fori_loop(0, groups, lambda g, c: (start_group(h_ref, l_ref, buf, g), c)[1], 0)

    def wait_tile(buf):
        def group(g, carry):
            for _ in range(SUBLANES * TOP_K_INNER):
                row_copy(buf, 0, 0, 0, 0, 0).wait()
            return carry
        lax.fori_loop(0, groups, group, 0)

    @pl.when(i == 0)
    def _():
        start_tile(hi_ref, lo_ref, 0)

    @pl.when(i + 1 < n)
    def _():
        start_tile(hi_nxt_ref, lo_nxt_ref, 1 - cur)

    wait_tile(cur)
    rt = route_ref[...]
    w = rows.shape[-1]
    y = (rt[:, 0:1] * _unpack_pairs(rows[cur, 0].reshape(tm, w))
         + rt[:, 1:2] * _unpack_pairs(rows[cur, 1].reshape(tm, w)))
    x2 = x_ref[...] + g2_ref[0] * y
    o_ref[...] = x2
    if next_ln:
        yn = x2 * lax.rsqrt(jnp.mean(x2 * x2, -1, keepdims=True) + RMS_EPS)
        h = ((yn * ng_ref[...]) * (1.0 + sc_ref[0]) + sh_ref[0]).astype(BF16)
        pr_ref[...], pd_ref[...], pm_ref[...] = _project(h, w_ref, gains_ref)


def _combine(yb, dest_flat, route, x1, g2, seq, next_ln=None, tm=512):
    T, D = x1.shape
    W = yb.shape[1]
    per = seq // tm
    B = T // seq
    n = T // tm
    tok = lambda w: pl.BlockSpec((tm, w), lambda i: (i, 0))
    vec = pl.BlockSpec((1, 1, D), lambda i: (i // per, 0, 0))
    S8 = SUBLANES
    smem_cur = pl.BlockSpec((TOP_K_INNER * tm,), lambda i: (i,), memory_space=pltpu.SMEM)
    smem_nxt = pl.BlockSpec((TOP_K_INNER * tm,), lambda i: (jnp.minimum(i + 1, n - 1),),
                            memory_space=pltpu.SMEM)
    in_specs = [smem_cur, smem_cur, smem_nxt, smem_nxt,
                pl.BlockSpec(memory_space=pl.ANY), tok(LANES), tok(D), vec]
    d_hi = dest_flat // S8
    d_lo = dest_flat % S8
    args = [d_hi, d_lo, d_hi, d_lo, yb.reshape(yb.shape[0] // S8, S8, W), route, x1, g2.reshape(B, 1, D)]
    out_specs = [tok(D)]
    out_shape = [jax.ShapeDtypeStruct((T, D), F32)]
    if next_ln is not None:
        ng, sc, sh, w_bf16, gains = next_ln
        in_specs += [pl.BlockSpec((1, D), lambda i: (0, 0)), vec, vec,
                     pl.BlockSpec((D, IN_COLS), lambda i: (0, 0)),
                     pl.BlockSpec(gains.shape, lambda i: (0, 0))]
        args += [ng.reshape(1, D), sc.reshape(B, 1, D), sh.reshape(B, 1, D), w_bf16, gains]
        out_specs += [tok(R_COLS), tok(DF_COLS), tok(MB_COLS)]
        out_shape += [jax.ShapeDtypeStruct((T, R_COLS), F32), jax.ShapeDtypeStruct((T, DF_COLS), BF16),
                      jax.ShapeDtypeStruct((T, MB_COLS), BF16)]
    return pl.pallas_call(
        functools.partial(_combine_kernel, next_ln=next_ln is not None),
        grid=(n,),
        in_specs=in_specs,
        out_specs=out_specs,
        out_shape=out_shape,
        scratch_shapes=[pltpu.VMEM((2, TOP_K_INNER, tm // S8, S8, W), jnp.uint32),
                        pltpu.SemaphoreType.DMA((2,))],
        compiler_params=_cparams(("arbitrary",)),
        name="moe_combine_ln" if next_ln is not None else "moe_combine",
    )(*args)


def _moe(h2p, route, x1, g2, w1, w3, w2, seq, layer, next_ln=None):
    T, D = x1.shape
    A = T * TOP_K_INNER
    n_blocks = -(-(A + N_EXPERTS * (MOE_BLOCK - 1)) // MOE_BLOCK)
    rank, cnt = _ranks(route)
    counts = cnt[0, :N_EXPERTS].astype(jnp.int32)
    padded = (counts + MOE_BLOCK - 1) // MOE_BLOCK * MOE_BLOCK
    pad_end = jnp.cumsum(padded)
    pad_start = pad_end - padded
    experts = route[:, 2:2 + TOP_K_INNER].astype(jnp.int32)
    is_e = experts[:, :, None] == jnp.arange(N_EXPERTS, dtype=jnp.int32)
    dest = (jnp.sum(jnp.where(is_e, pad_start, 0), axis=-1) + rank[:, :TOP_K_INNER].astype(jnp.int32)).reshape(A)
    blk_first = jnp.arange(n_blocks, dtype=jnp.int32) * MOE_BLOCK
    blk_e = jnp.minimum(jnp.sum((pad_end[None, :] <= blk_first[:, None]).astype(jnp.int32), axis=1),
                        N_EXPERTS - 1)
    n_used = (pad_end[-1:] // MOE_BLOCK).astype(jnp.int32)
    buf = _dispatch(h2p, dest, n_blocks * MOE_BLOCK)
    yb = _experts(buf, blk_e, counts, n_used, w1, w3, w2, layer)
    return _combine(yb, dest, route, x1, g2, seq, next_ln)


def kernel(x, c, ada_w, ada_b, norm1_g, norm2_g, w_in, w_out, rwkv_mu, rwkv_w0, rwkv_w2, rwkv_a0, rwkv_a2, rwkv_g2, rwkv_kk, rwkv_ka, rwkv_rk, rwkv_ln_g, rwkv_ln_b, diff_q_gain, diff_k_gain, diff_lambda, diff_subln_g, moba_q_gain, moba_k_gain, rel_bias, router_g_w, router_g_b, router_e_w, router_e_b, moe_w1, moe_w3, moe_w2):
    B, S, D = x.shape
    T = B * S
    mods = _adaln(c, ada_w, ada_b)
    bias_df = _bias_tiles(rel_bias[:, :DF_HEADS])
    bias_mb = _bias_tiles(rel_bias[:, DF_HEADS:])
    n_route = N_GROUPS + N_EXPERTS
    mod = lambda l, j: mods[l, :, j * D:(j + 1) * D]
    def qk_gains(l):
        two = lambda g, s: jnp.concatenate([g, g]) * s
        return jnp.stack([two(diff_q_gain[l], QK_SCALE), two(diff_k_gain[l], 1.0),
                          two(moba_q_gain[l], QK_SCALE), two(moba_k_gain[l], 1.0)])

    p_r, p_d, p_m = _ln_inproj(x, norm1_g[0], mod(0, 1), mod(0, 0), w_in[0].astype(BF16), qk_gains(0))
    for l in range(DEPTH):
        g1, sh2, sc2, g2 = mod(l, 2), mod(l, 3), mod(l, 4), mod(l, 5)
        r, k, v, kap, b, lw, g, bonus = _rwkv_prep(p_r, rwkv_mu[l], rwkv_w0[l], rwkv_w2[l], rwkv_a0[l],
                                                   rwkv_a2[l], rwkv_g2[l], rwkv_kk[l], rwkv_ka[l],
                                                   rwkv_rk[l])
        y_r = _rwkv_chunk(r, k, v, kap, b, lw, g, bonus, rwkv_ln_g[l], rwkv_ln_b[l])
        lambda_init = 0.8 - 0.6 * math.exp(-0.3 * l)
        y_d = _diff_attn(p_d, bias_df, diff_lambda[l], diff_subln_g[l], lambda_init)
        y_m = _moba(p_m, bias_mb)
        w_router = jnp.zeros((D, LANES), F32).at[:, :n_route].set(
            jnp.concatenate([router_g_w[l], router_e_w[l]], axis=1))
        b_router = jnp.zeros((1, LANES), F32).at[0, :n_route].set(
            jnp.concatenate([router_g_b[l], router_e_b[l]]))
        x1, h2p, route = _outproj(y_r, y_d, y_m, x, g1, sc2, sh2, norm2_g[l], w_out[l].astype(BF16),
                                  w_router, b_router)
        moe_args = (h2p.reshape(T, D // 2), route.reshape(T, LANES), x1.reshape(T, D), g2,
                    moe_w1, moe_w3, moe_w2, S, l)
        if l + 1 < DEPTH:
            nxt = (norm1_g[l + 1], mod(l + 1, 1), mod(l + 1, 0), w_in[l + 1].astype(BF16), qk_gains(l + 1))
            x2, p_r, p_d, p_m = _moe(*moe_args, next_ln=nxt)
            p_r, p_d, p_m = (p_r.reshape(B, S, R_COLS), p_d.reshape(B, S, DF_COLS),
                             p_m.reshape(B, S, MB_COLS))
        else:
            (x2,) = _moe(*moe_args)
        x = x2.reshape(B, S, D)
    return x
```

```python
import functools
import math

import jax
import jax.numpy as jnp
from jax import lax
from jax.experimental import pallas as pl
from jax.experimental.pallas import tpu as pltpu

F32 = jnp.float32
BF16 = jnp.bfloat16
HIGHEST = lax.Precision.HIGHEST

D_MODEL = 1024
DEPTH = 4
HEAD_DIM = 64
R_HEADS = 4
R_WIDTH = 256
R_LORA_W = 32
R_LORA_A = 32
R_LORA_G = 64
R_COLS = 896
R_GN_EPS = 64e-5
DF_HEADS = 4
DF_V = 128
DF_WIDTH = 512
DF_COLS = 1536
MB_HEADS = 4
MB_WIDTH = 256
MB_COLS = 768
MB_BLOCK = 256
MB_TOPK = 3
IN_COLS = 3200
NUM_BUCKETS = 32
MAX_DISTANCE = 1024
N_GROUPS = 4
EXPERTS_PER_GROUP = 8
N_EXPERTS = 32
TOP_K_INNER = 2
EXPERT_HIDDEN = 512
MOE_BLOCK = 256
RMS_EPS = 1e-6

LANES = 128
SUBLANES = 8
ATT_TILE = 256
SAT_TILE = 5
MASK_TILE = 6
N_BIAS_TILES = 7
ATT_UNROLL = 2
Q_HALVES = 2
RW_CHUNK = 64
RW_CHUNKS_PER_STEP = 8
NEG_BIG = -1e30
LOG2E = math.log2(math.e)
QK_SCALE = HEAD_DIM ** -0.5 * LOG2E
VMEM_LIMIT = 48 * 1024 * 1024


def _cparams(sem):
    return pltpu.CompilerParams(dimension_semantics=sem, vmem_limit_bytes=VMEM_LIMIT)


def _mm(a, b):
    return jnp.dot(a.astype(BF16), b.astype(BF16), preferred_element_type=F32)


def _mm_nt(a, b):
    return lax.dot_general(a.astype(BF16), b.astype(BF16), (((1,), (1,)), ((), ())),
                           preferred_element_type=F32)


def _mm_tn(a, b):
    return lax.dot_general(a.astype(BF16), b.astype(BF16), (((0,), (0,)), ((), ())),
                           preferred_element_type=F32)


def _mm_hi(a, b):
    return jnp.dot(a, b, precision=HIGHEST, preferred_element_type=F32)


def _mm_3pass(a, b):
    a_hi = a.astype(BF16)
    a_lo = (a - a_hi.astype(F32)).astype(BF16)
    b_hi = b.astype(BF16)
    b_lo = (b - b_hi.astype(F32)).astype(BF16)
    m = a.shape[0]
    first = jnp.dot(jnp.concatenate([a_hi, a_lo], axis=0), b_hi, preferred_element_type=F32)
    return first[:m] + first[m:] + jnp.dot(a_hi, b_lo, preferred_element_type=F32)


def _split_bf16(x, terms):
    parts = []
    for _ in range(terms):
        p = x.astype(BF16)
        parts.append(p)
        x = x - p.astype(F32)
    return parts


def _mm_exact_rhs(a, b01, terms=2):
    m = a.shape[0]
    out = jnp.dot(jnp.concatenate(_split_bf16(a, terms), axis=0), b01.astype(BF16), preferred_element_type=F32)
    return functools.reduce(lambda x, y: x + y, [out[t * m:(t + 1) * m] for t in range(terms)])


def _mm_exact_lhs(a01, b, terms=3):
    n = b.shape[1]
    out = jnp.dot(a01.astype(BF16), jnp.concatenate(_split_bf16(b, terms), axis=1), preferred_element_type=F32)
    return functools.reduce(lambda x, y: x + y, [out[:, t * n:(t + 1) * n] for t in range(terms)])


def _adaln_kernel(c_ref, w_ref, b_ref, o_ref):
    c = c_ref[...]
    cs = c * jax.nn.sigmoid(c)
    o_ref[0] = _mm_hi(cs, w_ref[0]) + b_ref[0]


def _adaln(c, ada_w, ada_b):
    L, D, D6 = ada_w.shape
    B = c.shape[0]
    nj = D6 // D
    return pl.pallas_call(
        _adaln_kernel,
        grid=(L, nj),
        in_specs=[pl.BlockSpec((B, D), lambda l, j: (0, 0)),
                  pl.BlockSpec((1, D, D), lambda l, j: (l, 0, j)),
                  pl.BlockSpec((1, 1, D), lambda l, j: (l, 0, j))],
        out_specs=pl.BlockSpec((1, B, D), lambda l, j: (l, 0, j)),
        out_shape=jax.ShapeDtypeStruct((L, B, D6), F32),
        compiler_params=_cparams(("parallel", "parallel")),
        name="adaln",
    )(c, ada_w, ada_b.reshape(L, 1, D6))


def _norm_qk_cols(p, n_qk, gains_ref, row):
    per = n_qk // LANES
    chunks = []
    for c in range(p.shape[1] // LANES):
        x = p[:, c * LANES:(c + 1) * LANES]
        if c < 2 * per:
            x = _rms_halves(x, gains_ref[row + c // per:row + c // per + 1, :])
        chunks.append(x)
    return jnp.concatenate(chunks, axis=1).astype(BF16)


def _project(h, w_ref, gains_ref):
    c1 = R_COLS
    c2 = R_COLS + DF_COLS
    p_r = jnp.dot(h, w_ref[:, :c1], preferred_element_type=F32)
    p_d = _norm_qk_cols(jnp.dot(h, w_ref[:, c1:c2], preferred_element_type=F32), DF_WIDTH, gains_ref, 0)
    p_m = _norm_qk_cols(jnp.dot(h, w_ref[:, c2:], preferred_element_type=F32), MB_WIDTH, gains_ref, 2)
    return p_r, p_d, p_m


def _ln_inproj_kernel(x_ref, g_ref, sc_ref, sh_ref, w_ref, gains_ref, pr_ref, pd_ref, pm_ref):
    x = x_ref[0]
    y = x * lax.rsqrt(jnp.mean(x * x, -1, keepdims=True) + RMS_EPS)
    h = ((y * g_ref[...]) * (1.0 + sc_ref[0]) + sh_ref[0]).astype(BF16)
    pr_ref[0], pd_ref[0], pm_ref[0] = _project(h, w_ref, gains_ref)


def _ln_inproj(x, g, sc, sh, w_bf16, gains, tm=512):
    B, S, D = x.shape
    return pl.pallas_call(
        _ln_inproj_kernel,
        grid=(B, S // tm),
        in_specs=[pl.BlockSpec((1, tm, D), lambda b, i: (b, i, 0)),
                  pl.BlockSpec((1, D), lambda b, i: (0, 0)),
                  pl.BlockSpec((1, 1, D), lambda b, i: (b, 0, 0)),
                  pl.BlockSpec((1, 1, D), lambda b, i: (b, 0, 0)),
                  pl.BlockSpec((D, IN_COLS), lambda b, i: (0, 0)),
                  pl.BlockSpec(gains.shape, lambda b, i: (0, 0))],
        out_specs=[pl.BlockSpec((1, tm, R_COLS), lambda b, i: (b, i, 0)),
                   pl.BlockSpec((1, tm, DF_COLS), lambda b, i: (b, i, 0)),
                   pl.BlockSpec((1, tm, MB_COLS), lambda b, i: (b, i, 0))],
        out_shape=[jax.ShapeDtypeStruct((B, S, R_COLS), F32),
                   jax.ShapeDtypeStruct((B, S, DF_COLS), BF16),
                   jax.ShapeDtypeStruct((B, S, MB_COLS), BF16)],
        compiler_params=_cparams(("parallel", "parallel")),
        name="ln_inproj",
    )(x, g.reshape(1, D), sc.reshape(B, 1, D), sh.reshape(B, 1, D), w_bf16, gains)


def _head_ones(n):
    r = lax.broadcasted_iota(jnp.int32, (n, n), 0) // HEAD_DIM
    c = lax.broadcasted_iota(jnp.int32, (n, n), 1) // HEAD_DIM
    return (r == c).astype(F32)


def _rwkv_prep_kernel(p_ref, pp_ref, mu_ref, w0_ref, w2_ref, a0_ref, a2_ref, g2_ref, kk_ref, ka_ref,
                      rk_ref, r_o, k_o, v_o, kap_o, b_o, lw_o, g_o, bonus_o):
    i = pl.program_id(1)
    p = p_ref[0]
    prev_row = jnp.where(i > 0, pp_ref[0][7:8, :], 0.0)
    rows = lax.broadcasted_iota(jnp.int32, p.shape, 0)
    p_prev = jnp.where(rows == 0, prev_row, pltpu.roll(p, 1, 0))
    ps = p + (p_prev - p) * mu_ref[...]
    W = R_WIDTH
    r = ps[:, 0:W]
    k = ps[:, W:2 * W]
    v = ps[:, 2 * W:3 * W]
    c3 = 3 * W
    c4 = c3 + R_LORA_W
    c5 = c4 + R_LORA_A
    wd = ps[:, c3:c4]
    ad = ps[:, c4:c5]
    gd = ps[:, c5:R_COLS]
    z = -(w0_ref[...] + _mm_3pass(jnp.tanh(wd), w2_ref[...]))
    softplus = jnp.maximum(z, 0.0) + jnp.log(1.0 + jnp.exp(-jnp.abs(z)))
    lw_o[0] = -jnp.exp(-softplus - 0.5)
    a = jax.nn.sigmoid(a0_ref[...] + _mm_3pass(ad, a2_ref[...]))
    g_o[0] = _mm_3pass(jax.nn.sigmoid(gd), g2_ref[...])
    ones = _head_ones(W)
    kk = k * kk_ref[...]
    nrm = jnp.sqrt(_mm_exact_rhs(kk * kk, ones))
    kap = kk / jnp.maximum(nrm, 1e-12)
    k2 = k * (1.0 + (a - 1.0) * ka_ref[...])
    r_o[0] = r
    k_o[0] = k2
    v_o[0] = v
    kap_o[0] = kap
    b_o[0] = kap * a
    bonus_o[0] = _mm_exact_rhs(r * k2 * rk_ref[...], ones) * v


def _rwkv_prep(p_r, mu, w0, w2, a0, a2, g2, k_k, k_a, r_k, tm=512):
    B, S, _ = p_r.shape
    W = R_WIDTH
    row = lambda a: a.reshape(1, -1)
    full = lambda a: pl.BlockSpec(a.shape, lambda b, i: (0,) * a.ndim)
    ins = [row(mu), row(w0), w2, row(a0), a2, g2, row(k_k), row(k_a), row(r_k)]
    out_spec = pl.BlockSpec((1, tm, W), lambda b, i: (b, i, 0))
    return pl.pallas_call(
        _rwkv_prep_kernel,
        grid=(B, S // tm),
        in_specs=[pl.BlockSpec((1, tm, R_COLS), lambda b, i: (b, i, 0)),
                  pl.BlockSpec((1, 8, R_COLS), lambda b, i: (b, jnp.maximum(i * (tm // 8) - 1, 0), 0))]
                 + [full(a) for a in ins],
        out_specs=[out_spec] * 8,
        out_shape=[jax.ShapeDtypeStruct((B, S, W), F32)] * 8,
        compiler_params=_cparams(("parallel", "arbitrary")),
        name="rwkv_prep",
    )(p_r, p_r, *ins)


def _rwkv_chunk_kernel(r_ref, k_ref, v_ref, kap_ref, b_ref, lw_ref, g_ref, bonus_ref, lng_ref, lnb_ref,
                       o_ref, state):
    step = pl.program_id(1)
    C = RW_CHUNK
    NC = RW_CHUNKS_PER_STEP
    Dh = HEAD_DIM
    TM = NC * C

    @pl.when(step == 0)
    def _():
        state[...] = jnp.zeros_like(state)

    ti = lax.broadcasted_iota(jnp.int32, (C, C), 0)
    tj = lax.broadcasted_iota(jnp.int32, (C, C), 1)
    eye = (ti == tj).astype(F32)
    ti2 = lax.broadcasted_iota(jnp.int32, (C, 2 * C), 0)
    tj2 = lax.broadcasted_iota(jnp.int32, (C, 2 * C), 1) % C
    strict2 = tj2 < ti2
    incl2 = tj2 <= ti2

    bi = lax.broadcasted_iota(jnp.int32, (TM, TM), 0)
    bj = lax.broadcasted_iota(jnp.int32, (TM, TM), 1)
    chunk_tri = jnp.logical_and(bi // C == bj // C, bj <= bi).astype(F32)

    lw = lw_ref[0]
    cum = _mm_exact_lhs(chunk_tri, lw)
    cum_end = jnp.concatenate(
        [jnp.broadcast_to(cum[(c + 1) * C - 1:(c + 1) * C, :], (C, R_WIDTH)) for c in range(NC)], axis=0)
    e_pos = jnp.exp(cum)
    e_neg = jnp.exp(-cum)
    e_prev = jnp.exp(cum - lw)
    e_end = jnp.exp(cum_end - cum)
    g_end = jnp.exp(cum_end)
    r = r_ref[0]
    k = k_ref[0]
    v = v_ref[0]
    b = b_ref[0]
    rt = r * e_pos
    kapt = kap_ref[0] * e_prev
    bt = b * e_neg
    kt = k * e_neg
    bh = b * e_end
    kh = k * e_end

    items = [(c, h) for c in range(NC) for h in range(R_HEADS)]

    def tile(x, c, h):
        return x[c * C:(c + 1) * C, h * Dh:(h + 1) * Dh]

    a_top, a_bot = [], []
    for c, h in items:
        lhs = jnp.concatenate([tile(kapt, c, h), tile(rt, c, h)], axis=0)
        rhs = jnp.concatenate([tile(bt, c, h), tile(kt, c, h)], axis=0)
        a = _mm_nt(lhs, rhs)
        a_top.append(jnp.where(strict2, a[:C], 0.0))
        a_bot.append(jnp.where(incl2, a[C:], 0.0))
    xp = [-t[:, :C] for t in a_top]
    tinv = [eye + x for x in xp]
    xp = [_mm(x, x) for x in xp]
    for _ in range(4):
        prod = [_mm(jnp.concatenate([t, x], axis=0), x) for t, x in zip(tinv, xp)]
        tinv = [t + p[:C] for t, p in zip(tinv, prod)]
        xp = [p[C:] for p in prod]
    tinv = [t + _mm(t, x) for t, x in zip(tinv, xp)]
    vt = [tile(v, c, h) for c, h in items]
    lkv = [_mm(t[:, C:], vv) for t, vv in zip(a_top, vt)]
    w = [_mm(t, tile(kapt, c, h)) for t, (c, h) in zip(tinv, items)]
    ploc = [-_mm(t, x) for t, x in zip(tinv, lkv)]
    q_eff = [tile(rt, c, h) - _mm(ab[:, :C], ww) for ab, ww, (c, h) in zip(a_bot, w, items)]
    pv = [jnp.concatenate([p, vv], axis=0) for p, vv in zip(ploc, vt)]
    y_loc = [_mm(ab, x) for ab, x in zip(a_bot, pv)]
    g_mat = [eye * tile(g_end, c, h)[0:1, :] - _mm_tn(ww, tile(bh, c, h)) for ww, (c, h) in zip(w, items)]
    u_mat = [_mm_tn(x, jnp.concatenate([tile(bh, c, h), tile(kh, c, h)], axis=0))
             for x, (c, h) in zip(pv, items)]

    s_cur = [state[h] for h in range(R_HEADS)]
    ys = [[None] * NC for _ in range(R_HEADS)]
    for n, (c, h) in enumerate(items):
        ys[h][c] = _mm_nt(q_eff[n], s_cur[h]) + y_loc[n]
        s_cur[h] = _mm(s_cur[h], g_mat[n]) + u_mat[n]
    for h in range(R_HEADS):
        state[h] = s_cur[h]

    gate = g_ref[0]
    bonus = bonus_ref[0]
    lng = lng_ref[...]
    lnb = lnb_ref[...]
    lane = lax.broadcasted_iota(jnp.int32, (Dh, R_WIDTH), 1)
    sub = lax.broadcasted_iota(jnp.int32, (Dh, R_WIDTH), 0)
    out = jnp.zeros((TM, R_WIDTH), F32)
    for h in range(R_HEADS):
        sl = slice(h * Dh, (h + 1) * Dh)
        y = jnp.concatenate(ys[h], axis=0)
        mean = jnp.mean(y, -1, keepdims=True)
        yc = y - mean
        var = jnp.mean(yc * yc, -1, keepdims=True)
        yn = yc * lax.rsqrt(var + R_GN_EPS)
        yfin = (yn * lng[:, sl] + lnb[:, sl] + bonus[:, sl]) * gate[:, sl]
        place = (lane == sub + h * Dh).astype(BF16)
        out = out + jnp.dot(yfin.astype(BF16), place, preferred_element_type=F32)
    o_ref[0] = out.astype(BF16)


def _rwkv_chunk(r, k, v, kap, b, lw, g, bonus, ln_g, ln_b):
    B, S, W = r.shape
    tm = RW_CHUNK * RW_CHUNKS_PER_STEP
    spec = pl.BlockSpec((1, tm, W), lambda bb, c: (bb, c, 0))
    vec = pl.BlockSpec((1, W), lambda bb, c: (0, 0))
    return pl.pallas_call(
        _rwkv_chunk_kernel,
        grid=(B, S // tm),
        in_specs=[spec] * 8 + [vec, vec],
        out_specs=spec,
        out_shape=jax.ShapeDtypeStruct((B, S, W), BF16),
        scratch_shapes=[pltpu.VMEM((R_HEADS, HEAD_DIM, HEAD_DIM), F32)],
        compiler_params=_cparams(("parallel", "arbitrary")),
        name="rwkv_chunk",
    )(r, k, v, kap, b, lw, g, bonus, ln_g.reshape(1, W), ln_b.reshape(1, W))


def _t5_bucket(dist):
    n = jnp.maximum(dist, 0)
    max_exact = NUM_BUCKETS // 2
    nf = jnp.maximum(n, 1).astype(F32)
    large = max_exact + (jnp.log(nf / max_exact) / math.log(MAX_DISTANCE / max_exact)
                         * (NUM_BUCKETS - max_exact)).astype(jnp.int32)
    large = jnp.minimum(large, NUM_BUCKETS - 1)
    return jnp.where(n < max_exact, n, large)


def _bias_tiles(tbl):
    T = ATT_TILE
    assert (SAT_TILE - 1) * T + 1 >= MAX_DISTANCE
    nh = tbl.shape[1]
    o = jnp.arange(SAT_TILE + 1)[:, None, None]
    i = jnp.arange(T)[None, :, None]
    j = jnp.arange(T)[None, None, :]
    dist = o * T + i - j
    onehot = (_t5_bucket(dist)[None] == jnp.arange(NUM_BUCKETS)[:, None, None, None]).astype(F32)
    tiles = jnp.dot(tbl.astype(F32).T, onehot.reshape(NUM_BUCKETS, -1), precision=HIGHEST)
    tiles = jnp.where((dist >= 0)[None], tiles.reshape(nh, SAT_TILE + 1, T, T) * LOG2E, NEG_BIG)
    return jnp.concatenate([tiles, jnp.full((nh, 1, T, T), NEG_BIG, F32)], axis=1)


def _half_masks(shape):
    lane = lax.broadcasted_iota(jnp.int32, shape, len(shape) - 1)
    lo = lane < HEAD_DIM
    return lo, jnp.logical_not(lo)


def _rms_halves(x, gain2):
    lo, hi = _half_masks(x.shape)
    sq = x * x
    s_lo = jnp.sum(jnp.where(lo, sq, 0.0), -1, keepdims=True)
    s_hi = jnp.sum(jnp.where(hi, sq, 0.0), -1, keepdims=True)
    inv = jnp.where(lo, lax.rsqrt(s_lo / HEAD_DIM + RMS_EPS), lax.rsqrt(s_hi / HEAD_DIM + RMS_EPS))
    return x * inv * gain2


def _lane_chunks(x):
    return [x[:, c * LANES:(c + 1) * LANES] for c in range(x.shape[1] // LANES)]


def _flash(q2, kn, v_ref, bias_tile, qi, s_bufs, m_s, l_s, acc_s):
    T = ATT_TILE
    U = ATT_UNROLL
    H = Q_HALVES
    n_iter = (qi * H + H - 1) // U + 1

    def window(i):
        return pl.ds(pl.multiple_of(jnp.minimum(i, n_iter - 1) * (U * T), U * T), U * T)

    def produce(i, dst):
        s = lax.dot_general(q2, kn[window(i), :], (((1,), (1,)), ((), ())), preferred_element_type=F32)
        idx = {}
        for d in range(-(U - 1), H):
            off = qi * H - i * U + d
            idx[d] = jnp.where(off < 0, MASK_TILE, jnp.minimum(off, SAT_TILE))
        bias = jnp.concatenate(
            [jnp.concatenate([bias_tile(g, idx[a - u]) for u in range(U)], axis=1)
             for g in range(2) for a in range(H)], axis=0)
        dst[...] = s + bias

    def consume(i, src):
        parts = _lane_chunks(src[...])
        m_old = m_s[...]
        blk_max = jnp.max(functools.reduce(jnp.maximum, parts), -1, keepdims=True)
        m_new = jnp.maximum(m_old, jnp.broadcast_to(blk_max, m_old.shape))
        alpha = jnp.exp2(m_old - m_new)
        ps = [jnp.exp2(part - m_new) for part in parts]
        l_s[...] = alpha * l_s[...] + functools.reduce(lambda a, b: a + b, ps)
        p = jnp.concatenate(ps, axis=1).astype(BF16)
        acc_s[...] = alpha * acc_s[...] + jnp.dot(p, v_ref[0, window(i), :], preferred_element_type=F32)
        m_s[...] = m_new

    m_s[...] = jnp.full_like(m_s, -jnp.inf)
    l_s[...] = jnp.zeros_like(l_s)
    acc_s[...] = jnp.zeros_like(acc_s)
    produce(0, s_bufs[0])

    def body(i, carry):
        for par in range(2):
            @pl.when(i % 2 == par)
            def _():
                produce(i + 1, s_bufs[1 - par])
                consume(i, s_bufs[par])
        return carry

    lax.fori_loop(0, n_iter, body, 0)
    return acc_s[...] / jnp.sum(l_s[...], -1, keepdims=True)


def _diff_attn_kernel(q_ref, k_ref, v_ref, bias_ref, lam_ref, sg_ref, o_ref,
                      s_a, s_b, m_s, l_s, acc_s, *, lambda_init):
    qi = pl.program_id(2)
    q = q_ref[0]
    tq = q.shape[0]
    lo, hi = _half_masks(q.shape)
    zero = jnp.zeros_like(q)
    q2 = jnp.concatenate([jnp.where(lo, q, zero), jnp.where(hi, q, zero)], axis=0)

    def bias_tile(g, idx):
        return bias_ref[0, idx]

    o = _flash(q2, k_ref.at[0], v_ref, bias_tile, qi, (s_a, s_b), m_s, l_s, acc_s)
    lam = lam_ref[...]
    lam_full = (jnp.exp(jnp.sum(lam[0:1] * lam[1:2], -1, keepdims=True))
                - jnp.exp(jnp.sum(lam[2:3] * lam[3:4], -1, keepdims=True)) + lambda_init)
    out = o[:tq] - lam_full * o[tq:]
    out = out * lax.rsqrt(jnp.mean(out * out, -1, keepdims=True) + RMS_EPS) * sg_ref[...]
    o_ref[0] = (out * (1.0 - lambda_init)).astype(BF16)


def _diff_attn(p_d, bias_tiles, lam, subln_g, lambda_init):
    B, S, _ = p_d.shape
    T = ATT_TILE
    H = DF_HEADS
    TQ = Q_HALVES * T
    R = 2 * TQ
    assert S % (ATT_UNROLL * T) == 0 and S % TQ == 0
    kern = functools.partial(_diff_attn_kernel, lambda_init=lambda_init)
    return pl.pallas_call(
        kern,
        grid=(B, H, S // TQ),
        in_specs=[pl.BlockSpec((1, TQ, LANES), lambda b, h, i: (b, i, h)),
                  pl.BlockSpec((1, S, LANES), lambda b, h, i: (b, 0, H + h)),
                  pl.BlockSpec((1, S, LANES), lambda b, h, i: (b, 0, 2 * H + h)),
                  pl.BlockSpec((1, N_BIAS_TILES, T, T), lambda b, h, i: (h, 0, 0, 0)),
                  pl.BlockSpec((4, HEAD_DIM), lambda b, h, i: (0, 0)),
                  pl.BlockSpec((1, DF_V), lambda b, h, i: (0, 0))],
        out_specs=pl.BlockSpec((1, TQ, DF_V), lambda b, h, i: (b, i, h)),
        out_shape=jax.ShapeDtypeStruct((B, S, DF_WIDTH), BF16),
        scratch_shapes=[pltpu.VMEM((R, ATT_UNROLL * T), F32),
                        pltpu.VMEM((R, ATT_UNROLL * T), F32),
                        pltpu.VMEM((R, LANES), F32),
                        pltpu.VMEM((R, LANES), F32),
                        pltpu.VMEM((R, DF_V), F32)],
        compiler_params=_cparams(("parallel", "parallel", "arbitrary")),
        name="diff_attn",
    )(p_d, p_d, p_d, bias_tiles, lam, subln_g.reshape(1, DF_V))


def _moba_kernel(q_ref, k_ref, v_ref, bias_ref, o_ref, kn, kmean, s_a, s_b, m_s, l_s, acc_s):
    qi = pl.program_id(2)
    T = ATT_TILE
    nkb = k_ref.shape[1] // T

    @pl.when(qi == 0)
    def _():
        kmean[...] = jnp.zeros_like(kmean)
        lane = lax.broadcasted_iota(jnp.int32, (T, LANES), 1)

        def body(j, carry):
            rows = pl.ds(pl.multiple_of(j * T, T), T)
            kb = k_ref[0, rows, :]
            kn[rows, :] = jnp.concatenate([kb, (lane == j).astype(BF16)], axis=1)
            kmean[pl.ds(j, 1), :] = jnp.mean(kb.astype(F32), 0, keepdims=True)
            return carry
        lax.fori_loop(0, nkb, body, 0)

    q = q_ref[0]
    tq = q.shape[0]
    lo, hi = _half_masks(q.shape)
    zero = jnp.zeros_like(q)
    q2b = jnp.concatenate([jnp.where(lo, q, zero), jnp.where(hi, q, zero)], axis=0)

    nkp = kmean.shape[0]
    g2 = lax.dot_general(jnp.concatenate(_split_bf16(kmean[...], 2), axis=0), q2b, (((1,), (1,)), ((), ())),
                         preferred_element_type=F32)
    gate = g2[:nkp] + g2[nkp:]
    blk = lax.broadcasted_iota(jnp.int32, gate.shape, 0).astype(F32)
    col = lax.broadcasted_iota(jnp.int32, gate.shape, 1)
    own = (qi * Q_HALVES + (col % tq) // T).astype(F32)
    past = blk < own
    gate = jnp.where(past, gate, -jnp.inf)
    pen_t = jnp.where(past, NEG_BIG, 0.0)
    for _ in range(MB_TOPK):
        mx = jnp.max(gate, 0, keepdims=True)
        first = jnp.min(jnp.where(gate == mx, blk, float(nkp)), 0, keepdims=True)
        pick = jnp.logical_and(blk == first, mx > -jnp.inf)
        pen_t = jnp.where(pick, 0.0, pen_t)
        gate = jnp.where(pick, -jnp.inf, gate)
    pen = jnp.concatenate([pen_t, jnp.zeros((LANES - nkp, 2 * tq), F32)], axis=0).T
    q2 = jnp.concatenate([q2b, pen.astype(BF16)], axis=1)

    def bias_tile(g, idx):
        return bias_ref[g, idx]

    o = _flash(q2, kn, v_ref, bias_tile, qi, (s_a, s_b), m_s, l_s, acc_s)
    lo_o, _ = _half_masks((tq, LANES))
    o_ref[0] = jnp.where(lo_o, o[:tq], o[tq:]).astype(BF16)


def _moba(p_m, bias_tiles):
    B, S, _ = p_m.shape
    T = ATT_TILE
    HP = MB_HEADS // 2
    TQ = Q_HALVES * T
    R = 2 * TQ
    assert T == MB_BLOCK and S // T <= LANES and S % (ATT_UNROLL * T) == 0 and S % TQ == 0
    return pl.pallas_call(
        _moba_kernel,
        grid=(B, HP, S // TQ),
        in_specs=[pl.BlockSpec((1, TQ, LANES), lambda b, h, i: (b, i, h)),
                  pl.BlockSpec((1, S, LANES), lambda b, h, i: (b, 0, HP + h)),
                  pl.BlockSpec((1, S, LANES), lambda b, h, i: (b, 0, 2 * HP + h)),
                  pl.BlockSpec((2, N_BIAS_TILES, T, T), lambda b, h, i: (h, 0, 0, 0))],
        out_specs=pl.BlockSpec((1, TQ, LANES), lambda b, h, i: (b, i, h)),
        out_shape=jax.ShapeDtypeStruct((B, S, MB_WIDTH), BF16),
        scratch_shapes=[pltpu.VMEM((S, 2 * LANES), BF16),
                        pltpu.VMEM((-(-(S // T) // 8) * 8, LANES), F32),
                        pltpu.VMEM((R, ATT_UNROLL * T), F32),
                        pltpu.VMEM((R, ATT_UNROLL * T), F32),
                        pltpu.VMEM((R, LANES), F32),
                        pltpu.VMEM((R, LANES), F32),
                        pltpu.VMEM((R, LANES), F32)],
        compiler_params=_cparams(("parallel", "parallel", "arbitrary")),
        name="moba",
    )(p_m, p_m, p_m, bias_tiles)


def _pack_pairs(x):
    n = x.shape[1] // 2
    hi = pltpu.bitcast(x[:, :n].astype(BF16).astype(F32), jnp.uint32)
    lo = pltpu.bitcast(x[:, n:].astype(BF16).astype(F32), jnp.uint32)
    return hi | (lo >> 16)


def _unpack_pairs(p):
    hi = pltpu.bitcast(p & jnp.uint32(0xFFFF0000), F32)
    lo = pltpu.bitcast(p << 16, F32)
    return jnp.concatenate([hi, lo], axis=1)


def _outproj_kernel(yr_ref, yd_ref, ym_ref, x_ref, g1_ref, sc_ref, sh_ref, ng_ref, w_ref, wr_ref, br_ref,
                    x1_ref, h2_ref, route_ref):
    c1 = R_WIDTH
    c2 = R_WIDTH + DF_WIDTH
    mix = (jnp.dot(yr_ref[0], w_ref[:c1, :], preferred_element_type=F32)
           + jnp.dot(yd_ref[0], w_ref[c1:c2, :], preferred_element_type=F32)
           + jnp.dot(ym_ref[0], w_ref[c2:, :], preferred_element_type=F32))
    x1 = x_ref[0] + g1_ref[0] * mix
    x1_ref[0] = x1
    y = x1 * lax.rsqrt(jnp.mean(x1 * x1, -1, keepdims=True) + RMS_EPS)
    h2 = (y * ng_ref[...]) * (1.0 + sc_ref[0]) + sh_ref[0]
    h2_ref[0] = _pack_pairs(h2)

    logits = _mm_3pass(h2, wr_ref[...]) + br_ref[...]
    lane = lax.broadcasted_iota(jnp.int32, logits.shape, 1).astype(F32)
    far = float(LANES)
    is_g = lane < N_GROUPS
    gl = jnp.where(is_g, logits, -jnp.inf)
    gmax = jnp.max(gl, -1, keepdims=True)
    g_idx = jnp.min(jnp.where(gl == gmax, lane, far), -1, keepdims=True)
    pg_top = 1.0 / jnp.sum(jnp.where(is_g, jnp.exp(logits - gmax), 0.0), -1, keepdims=True)
    e_lo = N_GROUPS + EXPERTS_PER_GROUP * g_idx
    in_grp = jnp.logical_and(lane >= e_lo, lane < e_lo + EXPERTS_PER_GROUP)
    el = jnp.where(in_grp, logits, -jnp.inf)
    m1 = jnp.max(el, -1, keepdims=True)
    i1 = jnp.min(jnp.where(el == m1, lane, far), -1, keepdims=True)
    el2 = jnp.where(lane == i1, -jnp.inf, el)
    m2 = jnp.max(el2, -1, keepdims=True)
    i2 = jnp.min(jnp.where(el2 == m2, lane, far), -1, keepdims=True)
    e2 = jnp.exp(m2 - m1)
    gate1 = pg_top / (1.0 + e2)
    gate2 = gate1 * e2
    route = jnp.where(lane == 0, gate1, 0.0)
    route = jnp.where(lane == 1, gate2, route)
    route = jnp.where(lane == 2, i1 - N_GROUPS, route)
    route = jnp.where(lane == 3, i2 - N_GROUPS, route)
    route_ref[0] = route


def _outproj(y_r, y_d, y_m, x, g1, sc2, sh2, norm2_g, w_out_bf16, w_router, b_router, tm=512):
    B, S, D = x.shape
    vec = pl.BlockSpec((1, 1, D), lambda b, i: (b, 0, 0))
    tok = lambda w: pl.BlockSpec((1, tm, w), lambda b, i: (b, i, 0))
    full = lambda a: pl.BlockSpec(a.shape, lambda b, i: (0,) * a.ndim)
    return pl.pallas_call(
        _outproj_kernel,
        grid=(B, S // tm),
        in_specs=[tok(R_WIDTH), tok(DF_WIDTH), tok(MB_WIDTH), tok(D), vec, vec, vec,
                  pl.BlockSpec((1, D), lambda b, i: (0, 0)), full(w_out_bf16), full(w_router), full(b_router)],
        out_specs=[tok(D), tok(D // 2), tok(LANES)],
        out_shape=[jax.ShapeDtypeStruct((B, S, D), F32),
                   jax.ShapeDtypeStruct((B, S, D // 2), jnp.uint32),
                   jax.ShapeDtypeStruct((B, S, LANES), F32)],
        compiler_params=_cparams(("parallel", "parallel")),
        name="outproj_router",
    )(y_r, y_d, y_m, x, g1.reshape(B, 1, D), sc2.reshape(B, 1, D), sh2.reshape(B, 1, D),
      norm2_g.reshape(1, D), w_out_bf16, w_router, b_router)


def _rank_kernel(route_ref, rank_ref, cnt_ref, carry):
    i = pl.program_id(0)

    @pl.when(i == 0)
    def _():
        carry[...] = jnp.zeros_like(carry)

    rt = route_ref[...]
    tm = rt.shape[0]
    lane = lax.broadcasted_iota(jnp.int32, rt.shape, 1).astype(F32)
    oh0 = (lane == rt[:, 2:3]).astype(F32)
    oh1 = (lane == rt[:, 3:4]).astype(F32)
    oh = oh0 + oh1
    row = lax.broadcasted_iota(jnp.int32, (tm, tm), 0)
    col = lax.broadcasted_iota(jnp.int32, (tm, tm), 1)
    before = jnp.dot((col < row).astype(BF16), oh.astype(BF16), preferred_element_type=F32) + carry[...]
    r0 = jnp.sum(oh0 * before, -1, keepdims=True)
    r1 = jnp.sum(oh1 * before, -1, keepdims=True)
    rank_ref[...] = jnp.where(lane == 0, r0, jnp.where(lane == 1, r1, 0.0))
    carry[...] = carry[...] + jnp.sum(oh, 0, keepdims=True)
    cnt_ref[...] = carry[...]


def _ranks(route, tm=512):
    T = route.shape[0]
    return pl.pallas_call(
        _rank_kernel,
        grid=(T // tm,),
        in_specs=[pl.BlockSpec((tm, LANES), lambda i: (i, 0))],
        out_specs=[pl.BlockSpec((tm, LANES), lambda i: (i, 0)), pl.BlockSpec((1, LANES), lambda i: (0, 0))],
        out_shape=[jax.ShapeDtypeStruct((T, LANES), F32), jax.ShapeDtypeStruct((1, LANES), F32)],
        scratch_shapes=[pltpu.VMEM((1, LANES), F32)],
        compiler_params=_cparams(("arbitrary",)),
        name="moe_ranks",
    )(route)


def _dispatch_kernel(dest_ref, h_ref, buf_in, buf_out, sem):
    del buf_in
    tm = h_ref.shape[0]

    def row_copy(t, slot):
        return pltpu.make_async_copy(h_ref.at[pl.ds(t, 1)], buf_out.at[pl.ds(slot, 1)], sem)

    def issue(g, carry):
        base = pl.multiple_of(g * SUBLANES, SUBLANES)
        for r in range(SUBLANES):
            for k in range(TOP_K_INNER):
                row_copy(base + r, dest_ref[TOP_K_INNER * (base + r) + k]).start(priority=k % 2)
        return carry

    def drain(g, carry):
        for _ in range(SUBLANES * TOP_K_INNER):
            row_copy(0, 0).wait()
        return carry

    lax.fori_loop(0, tm // SUBLANES, issue, 0)
    lax.fori_loop(0, tm // SUBLANES, drain, 0)


def _dispatch(h2p, dest_flat, n_slots, tm=512):
    T, W = h2p.shape
    buf0 = jnp.zeros((n_slots, W), jnp.uint32)
    return pl.pallas_call(
        _dispatch_kernel,
        grid=(T // tm,),
        in_specs=[pl.BlockSpec((TOP_K_INNER * tm,), lambda i: (i,), memory_space=pltpu.SMEM),
                  pl.BlockSpec((tm, W), lambda i: (i, 0)),
                  pl.BlockSpec(memory_space=pl.ANY)],
        out_specs=pl.BlockSpec(memory_space=pl.ANY),
        out_shape=jax.ShapeDtypeStruct((n_slots, W), jnp.uint32),
        scratch_shapes=[pltpu.SemaphoreType.DMA],
        input_output_aliases={2: 0},
        compiler_params=_cparams(("arbitrary",)),
        name="moe_dispatch",
    )(dest_flat, h2p, buf0)


def _expert_kernel(be_ref, slot_ref, nxt_ref, nu_ref, x_ref, w1_hbm, w3_hbm, w2_hbm, o_ref,
                   w1_s, w3_s, w2_s, sems, *, layer):
    i = pl.program_id(0)
    used = i < nu_ref[0]
    e = be_ref[i]
    slot = slot_ref[i]
    run_start = jnp.logical_or(i == 0, e != be_ref[jnp.maximum(i - 1, 0)])

    def fetch(expert, s):
        return [pltpu.make_async_copy(w1_hbm.at[layer, expert], w1_s.at[s], sems.at[s, 0]),
                pltpu.make_async_copy(w3_hbm.at[layer, expert], w3_s.at[s], sems.at[s, 1]),
                pltpu.make_async_copy(w2_hbm.at[layer, expert], w2_s.at[s], sems.at[s, 2])]

    @pl.when(jnp.logical_and(used, run_start))
    def _():
        @pl.when(i == 0)
        def _():
            for cp in fetch(e, slot):
                cp.start()
        for cp in fetch(e, slot):
            cp.wait()
        nxt = nxt_ref[i]

        @pl.when(nxt >= 0)
        def _():
            for cp in fetch(nxt, 1 - slot):
                cp.start()

    @pl.when(used)
    def _():
        x = _unpack_pairs(x_ref[...]).astype(BF16)
        h1 = jnp.dot(x, w1_s[slot].astype(BF16), preferred_element_type=F32)
        h3 = jnp.dot(x, w3_s[slot].astype(BF16), preferred_element_type=F32)
        h = (h1 * jax.nn.sigmoid(h1) * h3).astype(BF16)
        o_ref[...] = _pack_pairs(jnp.dot(h, w2_s[slot].astype(BF16), preferred_element_type=F32))

    @pl.when(jnp.logical_not(used))
    def _():
        o_ref[...] = jnp.zeros_like(o_ref)


def _experts(buf, blk_e, counts, n_used, w1, w3, w2, layer):
    n_slots, W = buf.shape
    _, E, D, Hd = w1.shape
    nb = n_slots // MOE_BLOCK
    change = jnp.concatenate([jnp.ones((1,), jnp.int32), (blk_e[1:] != blk_e[:-1]).astype(jnp.int32)])
    slot = (jnp.cumsum(change) - 1) % 2
    ids = jnp.arange(E, dtype=jnp.int32)
    later = jnp.logical_and(ids[None, :] > ids[:, None], counts[None, :] > 0)
    nxt_e = jnp.min(jnp.where(later, ids[None, :], E), axis=1)
    nxt_e = jnp.where(nxt_e == E, -1, nxt_e)
    nxt = jnp.sum(jnp.where(blk_e[:, None] == ids[None, :], nxt_e[None, :], 0), axis=1)
    any_spec = pl.BlockSpec(memory_space=pl.ANY)
    grid_spec = pltpu.PrefetchScalarGridSpec(
        num_scalar_prefetch=4,
        grid=(nb,),
        in_specs=[pl.BlockSpec((MOE_BLOCK, W), lambda i, *_: (i, 0)), any_spec, any_spec, any_spec],
        out_specs=pl.BlockSpec((MOE_BLOCK, W), lambda i, *_: (i, 0)),
        scratch_shapes=[pltpu.VMEM((2, D, Hd), F32), pltpu.VMEM((2, D, Hd), F32),
                        pltpu.VMEM((2, Hd, D), F32), pltpu.SemaphoreType.DMA((2, 3))],
    )
    return pl.pallas_call(
        functools.partial(_expert_kernel, layer=layer),
        grid_spec=grid_spec,
        out_shape=jax.ShapeDtypeStruct((n_slots, W), jnp.uint32),
        compiler_params=_cparams(("arbitrary",)),
        name="moe_experts",
    )(blk_e, slot.astype(jnp.int32), nxt.astype(jnp.int32), n_used, buf, w1, w3, w2)


def _combine_kernel(dest_ref, dest_nxt_ref, yb_ref, route_ref, x_ref, g2_ref, *rest, next_ln):
    if next_ln:
        ng_ref, sc_ref, sh_ref, w_ref, gains_ref, o_ref, pr_ref, pd_ref, pm_ref, rows, sems = rest
    else:
        o_ref, rows, sems = rest
    i = pl.program_id(0)
    n = pl.num_programs(0)
    tm = x_ref.shape[0]
    groups = tm // SUBLANES
    cur = i % 2

    def row_copy(buf, t, k, src_row):
        return pltpu.make_async_copy(yb_ref.at[pl.ds(src_row, 1)], rows.at[buf, k, pl.ds(t, 1)], sems.at[buf])

    def start_tile(d_ref, buf):
        def group(g, carry):
            base = pl.multiple_of(g * SUBLANES, SUBLANES)
            for r in range(SUBLANES):
                for k in range(TOP_K_INNER):
                    row_copy(buf, base + r, k, d_ref[TOP_K_INNER * (base + r) + k]).start(priority=k % 2)
            return carry
        lax.fori_loop(0, groups, group, 0)

    def wait_tile(buf):
        def group(g, carry):
            for _ in range(SUBLANES * TOP_K_INNER):
                row_copy(buf, 0, 0, 0).wait()
            return carry
        lax.fori_loop(0, groups, group, 0)

    @pl.when(i == 0)
    def _():
        start_tile(dest_ref, 0)

    @pl.when(i + 1 < n)
    def _():
        start_tile(dest_nxt_ref, 1 - cur)

    wait_tile(cur)
    rt = route_ref[...]
    y = rt[:, 0:1] * _unpack_pairs(rows[cur, 0]) + rt[:, 1:2] * _unpack_pairs(rows[cur, 1])
    x2 = x_ref[...] + g2_ref[0] * y
    o_ref[...] = x2
    if next_ln:
        yn = x2 * lax.rsqrt(jnp.mean(x2 * x2, -1, keepdims=True) + RMS_EPS)
        h = ((yn * ng_ref[...]) * (1.0 + sc_ref[0]) + sh_ref[0]).astype(BF16)
        pr_ref[...], pd_ref[...], pm_ref[...] = _project(h, w_ref, gains_ref)


def _combine(yb, dest_flat, route, x1, g2, seq, next_ln=None, tm=512):
    T, D = x1.shape
    W = yb.shape[1]
    per = seq // tm
    B = T // seq
    n = T // tm
    tok = lambda w: pl.BlockSpec((tm, w), lambda i: (i, 0))
    vec = pl.BlockSpec((1, 1, D), lambda i: (i // per, 0, 0))
    smem_cur = pl.BlockSpec((TOP_K_INNER * tm,), lambda i: (i,), memory_space=pltpu.SMEM)
    smem_nxt = pl.BlockSpec((TOP_K_INNER * tm,), lambda i: (jnp.minimum(i + 1, n - 1),),
                            memory_space=pltpu.SMEM)
    in_specs = [smem_cur, smem_nxt, pl.BlockSpec(memory_space=pl.ANY), tok(LANES), tok(D), vec]
    args = [dest_flat, dest_flat, yb, route, x1, g2.reshape(B, 1, D)]
    out_specs = [tok(D)]
    out_shape = [jax.ShapeDtypeStruct((T, D), F32)]
    if next_ln is not None:
        ng, sc, sh, w_bf16, gains = next_ln
        in_specs += [pl.BlockSpec((1, D), lambda i: (0, 0)), vec, vec,
                     pl.BlockSpec((D, IN_COLS), lambda i: (0, 0)),
                     pl.BlockSpec(gains.shape, lambda i: (0, 0))]
        args += [ng.reshape(1, D), sc.reshape(B, 1, D), sh.reshape(B, 1, D), w_bf16, gains]
        out_specs += [tok(R_COLS), tok(DF_COLS), tok(MB_COLS)]
        out_shape += [jax.ShapeDtypeStruct((T, R_COLS), F32), jax.ShapeDtypeStruct((T, DF_COLS), BF16),
                      jax.ShapeDtypeStruct((T, MB_COLS), BF16)]
    return pl.pallas_call(
        functools.partial(_combine_kernel, next_ln=next_ln is not None),
        grid=(n,),
        in_specs=in_specs,
        out_specs=out_specs,
        out_shape=out_shape,
        scratch_shapes=[pltpu.VMEM((2, TOP_K_INNER, tm, W), jnp.uint32), pltpu.SemaphoreType.DMA((2,))],
        compiler_params=_cparams(("arbitrary",)),
        name="moe_combine_ln" if next_ln is not None else "moe_combine",
    )(*args)


def _moe(h2p, route, x1, g2, w1, w3, w2, seq, layer, next_ln=None):
    T, D = x1.shape
    A = T * TOP_K_INNER
    n_blocks = -(-(A + N_EXPERTS * (MOE_BLOCK - 1)) // MOE_BLOCK)
    rank, cnt = _ranks(route)
    counts = cnt[0, :N_EXPERTS].astype(jnp.int32)
    padded = (counts + MOE_BLOCK - 1) // MOE_BLOCK * MOE_BLOCK
    pad_end = jnp.cumsum(padded)
    pad_start = pad_end - padded
    experts = route[:, 2:2 + TOP_K_INNER].astype(jnp.int32)
    is_e = experts[:, :, None] == jnp.arange(N_EXPERTS, dtype=jnp.int32)
    dest = (jnp.sum(jnp.where(is_e, pad_start, 0), axis=-1) + rank[:, :TOP_K_INNER].astype(jnp.int32)).reshape(A)
    blk_first = jnp.arange(n_blocks, dtype=jnp.int32) * MOE_BLOCK
    blk_e = jnp.minimum(jnp.sum((pad_end[None, :] <= blk_first[:, None]).astype(jnp.int32), axis=1),
                        N_EXPERTS - 1)
    n_used = (pad_end[-1:] // MOE_BLOCK).astype(jnp.int32)
    buf = _dispatch(h2p, dest, n_blocks * MOE_BLOCK)
    yb = _experts(buf, blk_e, counts, n_used, w1, w3, w2, layer)
    return _combine(yb, dest, route, x1, g2, seq, next_ln)


def kernel(x, c, ada_w, ada_b, norm1_g, norm2_g, w_in, w_out, rwkv_mu, rwkv_w0, rwkv_w2, rwkv_a0, rwkv_a2, rwkv_g2, rwkv_kk, rwkv_ka, rwkv_rk, rwkv_ln_g, rwkv_ln_b, diff_q_gain, diff_k_gain, diff_lambda, diff_subln_g, moba_q_gain, moba_k_gain, rel_bias, router_g_w, router_g_b, router_e_w, router_e_b, moe_w1, moe_w3, moe_w2):
    B, S, D = x.shape
    T = B * S
    mods = _adaln(c, ada_w, ada_b)
    bias_df = _bias_tiles(rel_bias[:, :DF_HEADS])
    bias_mb = _bias_tiles(rel_bias[:, DF_HEADS:])
    n_route = N_GROUPS + N_EXPERTS
    mod = lambda l, j: mods[l, :, j * D:(j + 1) * D]
    def qk_gains(l):
        two = lambda g, s: jnp.concatenate([g, g]) * s
        return jnp.stack([two(diff_q_gain[l], QK_SCALE), two(diff_k_gain[l], 1.0),
                          two(moba_q_gain[l], QK_SCALE), two(moba_k_gain[l], 1.0)])

    p_r, p_d, p_m = _ln_inproj(x, norm1_g[0], mod(0, 1), mod(0, 0), w_in[0].astype(BF16), qk_gains(0))
    for l in range(DEPTH):
        g1, sh2, sc2, g2 = mod(l, 2), mod(l, 3), mod(l, 4), mod(l, 5)
        r, k, v, kap, b, lw, g, bonus = _rwkv_prep(p_r, rwkv_mu[l], rwkv_w0[l], rwkv_w2[l], rwkv_a0[l],
                                                   rwkv_a2[l], rwkv_g2[l], rwkv_kk[l], rwkv_ka[l],
                                                   rwkv_rk[l])
        y_r = _rwkv_chunk(r, k, v, kap, b, lw, g, bonus, rwkv_ln_g[l], rwkv_ln_b[l])
        lambda_init = 0.8 - 0.6 * math.exp(-0.3 * l)
        y_d = _diff_attn(p_d, bias_df, diff_lambda[l], diff_subln_g[l], lambda_init)
        y_m = _moba(p_m, bias_mb)
        w_router = jnp.zeros((D, LANES), F32).at[:, :n_route].set(
            jnp.concatenate([router_g_w[l], router_e_w[l]], axis=1))
        b_router = jnp.zeros((1, LANES), F32).at[0, :n_route].set(
            jnp.concatenate([router_g_b[l], router_e_b[l]]))
        x1, h2p, route = _outproj(y_r, y_d, y_m, x, g1, sc2, sh2, norm2_g[l], w_out[l].astype(BF16),
                                  w_router, b_router)
        moe_args = (h2p.reshape(T, D // 2), route.reshape(T, LANES), x1.reshape(T, D), g2,
                    moe_w1, moe_w3, moe_w2, S, l)
        if l + 1 < DEPTH:
            nxt = (norm1_g[l + 1], mod(l + 1, 1), mod(l + 1, 0), w_in[l + 1].astype(BF16), qk_gains(l + 1))
            x2, p_r, p_d, p_m = _moe(*moe_args, next_ln=nxt)
            p_r, p_d, p_m = (p_r.reshape(B, S, R_COLS), p_d.reshape(B, S, DF_COLS),
                             p_m.reshape(B, S, MB_COLS))
        else:
            (x2,) = _moe(*moe_args)
        x = x2.reshape(B, S, D)
    return x
```

```python
import functools
import math

import jax
import jax.numpy as jnp
from jax import lax
from jax.experimental import pallas as pl
from jax.experimental.pallas import tpu as pltpu

F32 = jnp.float32
BF16 = jnp.bfloat16
HIGHEST = lax.Precision.HIGHEST

D_MODEL = 1024
DEPTH = 4
HEAD_DIM = 64
R_HEADS = 4
R_WIDTH = 256
R_LORA_W = 32
R_LORA_A = 32
R_LORA_G = 64
R_COLS = 896
R_GN_EPS = 64e-5
DF_HEADS = 4
DF_V = 128
DF_WIDTH = 512
DF_COLS = 1536
MB_HEADS = 4
MB_WIDTH = 256
MB_COLS = 768
MB_BLOCK = 256
MB_TOPK = 3
IN_COLS = 3200
NUM_BUCKETS = 32
MAX_DISTANCE = 1024
N_GROUPS = 4
EXPERTS_PER_GROUP = 8
N_EXPERTS = 32
TOP_K_INNER = 2
EXPERT_HIDDEN = 512
MOE_BLOCK = 256
RMS_EPS = 1e-6

LANES = 128
SUBLANES = 8
ATT_TILE = 256
SAT_TILE = 5
MASK_TILE = 6
N_BIAS_TILES = 7
ATT_UNROLL = 2
Q_HALVES = 2
RW_CHUNK = 64
RW_CHUNKS_PER_STEP = 8
NEG_BIG = -1e30
LOG2E = math.log2(math.e)
QK_SCALE = HEAD_DIM ** -0.5 * LOG2E
VMEM_LIMIT = 48 * 1024 * 1024


def _cparams(sem):
    return pltpu.CompilerParams(dimension_semantics=sem, vmem_limit_bytes=VMEM_LIMIT)


def _mm(a, b):
    return jnp.dot(a.astype(BF16), b.astype(BF16), preferred_element_type=F32)


def _mm_nt(a, b):
    return lax.dot_general(a.astype(BF16), b.astype(BF16), (((1,), (1,)), ((), ())),
                           preferred_element_type=F32)


def _mm_tn(a, b):
    return lax.dot_general(a.astype(BF16), b.astype(BF16), (((0,), (0,)), ((), ())),
                           preferred_element_type=F32)


def _mm_hi(a, b):
    return jnp.dot(a, b, precision=HIGHEST, preferred_element_type=F32)


def _mm_3pass(a, b):
    a_hi = a.astype(BF16)
    a_lo = (a - a_hi.astype(F32)).astype(BF16)
    b_hi = b.astype(BF16)
    b_lo = (b - b_hi.astype(F32)).astype(BF16)
    m = a.shape[0]
    first = jnp.dot(jnp.concatenate([a_hi, a_lo], axis=0), b_hi, preferred_element_type=F32)
    return first[:m] + first[m:] + jnp.dot(a_hi, b_lo, preferred_element_type=F32)


def _split_bf16(x, terms):
    parts = []
    for _ in range(terms):
        p = x.astype(BF16)
        parts.append(p)
        x = x - p.astype(F32)
    return parts


def _mm_exact_rhs(a, b01, terms=2):
    m = a.shape[0]
    out = jnp.dot(jnp.concatenate(_split_bf16(a, terms), axis=0), b01.astype(BF16), preferred_element_type=F32)
    return functools.reduce(lambda x, y: x + y, [out[t * m:(t + 1) * m] for t in range(terms)])


def _mm_exact_lhs(a01, b, terms=3):
    n = b.shape[1]
    out = jnp.dot(a01.astype(BF16), jnp.concatenate(_split_bf16(b, terms), axis=1), preferred_element_type=F32)
    return functools.reduce(lambda x, y: x + y, [out[:, t * n:(t + 1) * n] for t in range(terms)])


def _adaln_kernel(c_ref, w_ref, b_ref, o_ref):
    c = c_ref[...]
    cs = c * jax.nn.sigmoid(c)
    o_ref[0] = _mm_hi(cs, w_ref[0]) + b_ref[0]


def _adaln(c, ada_w, ada_b):
    L, D, D6 = ada_w.shape
    B = c.shape[0]
    nj = D6 // D
    return pl.pallas_call(
        _adaln_kernel,
        grid=(L, nj),
        in_specs=[pl.BlockSpec((B, D), lambda l, j: (0, 0)),
                  pl.BlockSpec((1, D, D), lambda l, j: (l, 0, j)),
                  pl.BlockSpec((1, 1, D), lambda l, j: (l, 0, j))],
        out_specs=pl.BlockSpec((1, B, D), lambda l, j: (l, 0, j)),
        out_shape=jax.ShapeDtypeStruct((L, B, D6), F32),
        compiler_params=_cparams(("parallel", "parallel")),
        name="adaln",
    )(c, ada_w, ada_b.reshape(L, 1, D6))


def _norm_qk_cols(p, n_qk, gains_ref, row):
    per = n_qk // LANES
    chunks = []
    for c in range(p.shape[1] // LANES):
        x = p[:, c * LANES:(c + 1) * LANES]
        if c < 2 * per:
            x = _rms_halves(x, gains_ref[row + c // per:row + c // per + 1, :])
        chunks.append(x)
    return jnp.concatenate(chunks, axis=1).astype(BF16)


def _project(h, w_ref, gains_ref):
    c1 = R_COLS
    c2 = R_COLS + DF_COLS
    p_r = jnp.dot(h, w_ref[:, :c1], preferred_element_type=F32)
    p_d = _norm_qk_cols(jnp.dot(h, w_ref[:, c1:c2], preferred_element_type=F32), DF_WIDTH, gains_ref, 0)
    p_m = _norm_qk_cols(jnp.dot(h, w_ref[:, c2:], preferred_element_type=F32), MB_WIDTH, gains_ref, 2)
    return p_r, p_d, p_m


def _ln_inproj_kernel(x_ref, g_ref, sc_ref, sh_ref, w_ref, gains_ref, pr_ref, pd_ref, pm_ref):
    x = x_ref[0]
    y = x * lax.rsqrt(jnp.mean(x * x, -1, keepdims=True) + RMS_EPS)
    h = ((y * g_ref[...]) * (1.0 + sc_ref[0]) + sh_ref[0]).astype(BF16)
    pr_ref[0], pd_ref[0], pm_ref[0] = _project(h, w_ref, gains_ref)


def _ln_inproj(x, g, sc, sh, w_bf16, gains, tm=512):
    B, S, D = x.shape
    return pl.pallas_call(
        _ln_inproj_kernel,
        grid=(B, S // tm),
        in_specs=[pl.BlockSpec((1, tm, D), lambda b, i: (b, i, 0)),
                  pl.BlockSpec((1, D), lambda b, i: (0, 0)),
                  pl.BlockSpec((1, 1, D), lambda b, i: (b, 0, 0)),
                  pl.BlockSpec((1, 1, D), lambda b, i: (b, 0, 0)),
                  pl.BlockSpec((D, IN_COLS), lambda b, i: (0, 0)),
                  pl.BlockSpec(gains.shape, lambda b, i: (0, 0))],
        out_specs=[pl.BlockSpec((1, tm, R_COLS), lambda b, i: (b, i, 0)),
                   pl.BlockSpec((1, tm, DF_COLS), lambda b, i: (b, i, 0)),
                   pl.BlockSpec((1, tm, MB_COLS), lambda b, i: (b, i, 0))],
        out_shape=[jax.ShapeDtypeStruct((B, S, R_COLS), F32),
                   jax.ShapeDtypeStruct((B, S, DF_COLS), BF16),
                   jax.ShapeDtypeStruct((B, S, MB_COLS), BF16)],
        compiler_params=_cparams(("parallel", "parallel")),
        name="ln_inproj",
    )(x, g.reshape(1, D), sc.reshape(B, 1, D), sh.reshape(B, 1, D), w_bf16, gains)


def _head_ones(n):
    r = lax.broadcasted_iota(jnp.int32, (n, n), 0) // HEAD_DIM
    c = lax.broadcasted_iota(jnp.int32, (n, n), 1) // HEAD_DIM
    return (r == c).astype(F32)


def _rwkv_prep_kernel(p_ref, pp_ref, mu_ref, w0_ref, w2_ref, a0_ref, a2_ref, g2_ref, kk_ref, ka_ref,
                      rk_ref, r_o, k_o, v_o, kap_o, b_o, lw_o, g_o, bonus_o):
    i = pl.program_id(1)
    p = p_ref[0]
    prev_row = jnp.where(i > 0, pp_ref[0][7:8, :], 0.0)
    rows = lax.broadcasted_iota(jnp.int32, p.shape, 0)
    p_prev = jnp.where(rows == 0, prev_row, pltpu.roll(p, 1, 0))
    ps = p + (p_prev - p) * mu_ref[...]
    W = R_WIDTH
    r = ps[:, 0:W]
    k = ps[:, W:2 * W]
    v = ps[:, 2 * W:3 * W]
    c3 = 3 * W
    c4 = c3 + R_LORA_W
    c5 = c4 + R_LORA_A
    wd = ps[:, c3:c4]
    ad = ps[:, c4:c5]
    gd = ps[:, c5:R_COLS]
    z = -(w0_ref[...] + _mm_3pass(jnp.tanh(wd), w2_ref[...]))
    softplus = jnp.maximum(z, 0.0) + jnp.log(1.0 + jnp.exp(-jnp.abs(z)))
    lw_o[0] = -jnp.exp(-softplus - 0.5)
    a = jax.nn.sigmoid(a0_ref[...] + _mm_3pass(ad, a2_ref[...]))
    g_o[0] = _mm_3pass(jax.nn.sigmoid(gd), g2_ref[...])
    ones = _head_ones(W)
    kk = k * kk_ref[...]
    nrm = jnp.sqrt(_mm_exact_rhs(kk * kk, ones))
    kap = kk / jnp.maximum(nrm, 1e-12)
    k2 = k * (1.0 + (a - 1.0) * ka_ref[...])
    r_o[0] = r
    k_o[0] = k2
    v_o[0] = v
    kap_o[0] = kap
    b_o[0] = kap * a
    bonus_o[0] = _mm_exact_rhs(r * k2 * rk_ref[...], ones) * v


def _rwkv_prep(p_r, mu, w0, w2, a0, a2, g2, k_k, k_a, r_k, tm=512):
    B, S, _ = p_r.shape
    W = R_WIDTH
    row = lambda a: a.reshape(1, -1)
    full = lambda a: pl.BlockSpec(a.shape, lambda b, i: (0,) * a.ndim)
    ins = [row(mu), row(w0), w2, row(a0), a2, g2, row(k_k), row(k_a), row(r_k)]
    out_spec = pl.BlockSpec((1, tm, W), lambda b, i: (b, i, 0))
    return pl.pallas_call(
        _rwkv_prep_kernel,
        grid=(B, S // tm),
        in_specs=[pl.BlockSpec((1, tm, R_COLS), lambda b, i: (b, i, 0)),
                  pl.BlockSpec((1, 8, R_COLS), lambda b, i: (b, jnp.maximum(i * (tm // 8) - 1, 0), 0))]
                 + [full(a) for a in ins],
        out_specs=[out_spec] * 8,
        out_shape=[jax.ShapeDtypeStruct((B, S, W), F32)] * 8,
        compiler_params=_cparams(("parallel", "arbitrary")),
        name="rwkv_prep",
    )(p_r, p_r, *ins)


def _rwkv_chunk_kernel(r_ref, k_ref, v_ref, kap_ref, b_ref, lw_ref, g_ref, bonus_ref, lng_ref, lnb_ref,
                       o_ref, state):
    step = pl.program_id(1)
    C = RW_CHUNK
    NC = RW_CHUNKS_PER_STEP
    Dh = HEAD_DIM
    TM = NC * C

    @pl.when(step == 0)
    def _():
        state[...] = jnp.zeros_like(state)

    ti = lax.broadcasted_iota(jnp.int32, (C, C), 0)
    tj = lax.broadcasted_iota(jnp.int32, (C, C), 1)
    eye = (ti == tj).astype(F32)
    ti2 = lax.broadcasted_iota(jnp.int32, (C, 2 * C), 0)
    tj2 = lax.broadcasted_iota(jnp.int32, (C, 2 * C), 1) % C
    strict2 = tj2 < ti2
    incl2 = tj2 <= ti2

    bi = lax.broadcasted_iota(jnp.int32, (TM, TM), 0)
    bj = lax.broadcasted_iota(jnp.int32, (TM, TM), 1)
    chunk_tri = jnp.logical_and(bi // C == bj // C, bj <= bi).astype(F32)

    lw = lw_ref[0]
    cum = _mm_exact_lhs(chunk_tri, lw)
    cum_end = jnp.concatenate(
        [jnp.broadcast_to(cum[(c + 1) * C - 1:(c + 1) * C, :], (C, R_WIDTH)) for c in range(NC)], axis=0)
    e_pos = jnp.exp(cum)
    e_neg = jnp.exp(-cum)
    e_prev = jnp.exp(cum - lw)
    e_end = jnp.exp(cum_end - cum)
    g_end = jnp.exp(cum_end)
    r = r_ref[0]
    k = k_ref[0]
    v = v_ref[0]
    b = b_ref[0]
    rt = r * e_pos
    kapt = kap_ref[0] * e_prev
    bt = b * e_neg
    kt = k * e_neg
    bh = b * e_end
    kh = k * e_end

    items = [(c, h) for c in range(NC) for h in range(R_HEADS)]

    def tile(x, c, h):
        return x[c * C:(c + 1) * C, h * Dh:(h + 1) * Dh]

    a_top, a_bot = [], []
    for c, h in items:
        lhs = jnp.concatenate([tile(kapt, c, h), tile(rt, c, h)], axis=0)
        rhs = jnp.concatenate([tile(bt, c, h), tile(kt, c, h)], axis=0)
        a = _mm_nt(lhs, rhs)
        a_top.append(jnp.where(strict2, a[:C], 0.0))
        a_bot.append(jnp.where(incl2, a[C:], 0.0))
    xp = [-t[:, :C] for t in a_top]
    tinv = [eye + x for x in xp]
    xp = [_mm(x, x) for x in xp]
    for _ in range(4):
        prod = [_mm(jnp.concatenate([t, x], axis=0), x) for t, x in zip(tinv, xp)]
        tinv = [t + p[:C] for t, p in zip(tinv, prod)]
        xp = [p[C:] for p in prod]
    tinv = [t + _mm(t, x) for t, x in zip(tinv, xp)]
    vt = [tile(v, c, h) for c, h in items]
    lkv = [_mm(t[:, C:], vv) for t, vv in zip(a_top, vt)]
    w = [_mm(t, tile(kapt, c, h)) for t, (c, h) in zip(tinv, items)]
    ploc = [-_mm(t, x) for t, x in zip(tinv, lkv)]
    q_eff = [tile(rt, c, h) - _mm(ab[:, :C], ww) for ab, ww, (c, h) in zip(a_bot, w, items)]
    pv = [jnp.concatenate([p, vv], axis=0) for p, vv in zip(ploc, vt)]
    y_loc = [_mm(ab, x) for ab, x in zip(a_bot, pv)]
    g_mat = [eye * tile(g_end, c, h)[0:1, :] - _mm_tn(ww, tile(bh, c, h)) for ww, (c, h) in zip(w, items)]
    u_mat = [_mm_tn(x, jnp.concatenate([tile(bh, c, h), tile(kh, c, h)], axis=0))
             for x, (c, h) in zip(pv, items)]

    s_cur = [state[h] for h in range(R_HEADS)]
    ys = [[None] * NC for _ in range(R_HEADS)]
    for n, (c, h) in enumerate(items):
        ys[h][c] = _mm_nt(q_eff[n], s_cur[h]) + y_loc[n]
        s_cur[h] = _mm(s_cur[h], g_mat[n]) + u_mat[n]
    for h in range(R_HEADS):
        state[h] = s_cur[h]

    gate = g_ref[0]
    bonus = bonus_ref[0]
    lng = lng_ref[...]
    lnb = lnb_ref[...]
    lane = lax.broadcasted_iota(jnp.int32, (Dh, R_WIDTH), 1)
    sub = lax.broadcasted_iota(jnp.int32, (Dh, R_WIDTH), 0)
    out = jnp.zeros((TM, R_WIDTH), F32)
    for h in range(R_HEADS):
        sl = slice(h * Dh, (h + 1) * Dh)
        y = jnp.concatenate(ys[h], axis=0)
        mean = jnp.mean(y, -1, keepdims=True)
        yc = y - mean
        var = jnp.mean(yc * yc, -1, keepdims=True)
        yn = yc * lax.rsqrt(var + R_GN_EPS)
        yfin = (yn * lng[:, sl] + lnb[:, sl] + bonus[:, sl]) * gate[:, sl]
        place = (lane == sub + h * Dh).astype(BF16)
        out = out + jnp.dot(yfin.astype(BF16), place, preferred_element_type=F32)
    o_ref[0] = out.astype(BF16)


def _rwkv_chunk(r, k, v, kap, b, lw, g, bonus, ln_g, ln_b):
    B, S, W = r.shape
    tm = RW_CHUNK * RW_CHUNKS_PER_STEP
    spec = pl.BlockSpec((1, tm, W), lambda bb, c: (bb, c, 0))
    vec = pl.BlockSpec((1, W), lambda bb, c: (0, 0))
    return pl.pallas_call(
        _rwkv_chunk_kernel,
        grid=(B, S // tm),
        in_specs=[spec] * 8 + [vec, vec],
        out_specs=spec,
        out_shape=jax.ShapeDtypeStruct((B, S, W), BF16),
        scratch_shapes=[pltpu.VMEM((R_HEADS, HEAD_DIM, HEAD_DIM), F32)],
        compiler_params=_cparams(("parallel", "arbitrary")),
        name="rwkv_chunk",
    )(r, k, v, kap, b, lw, g, bonus, ln_g.reshape(1, W), ln_b.reshape(1, W))


def _t5_bucket(dist):
    n = jnp.maximum(dist, 0)
    max_exact = NUM_BUCKETS // 2
    nf = jnp.maximum(n, 1).astype(F32)
    large = max_exact + (jnp.log(nf / max_exact) / math.log(MAX_DISTANCE / max_exact)
                         * (NUM_BUCKETS - max_exact)).astype(jnp.int32)
    large = jnp.minimum(large, NUM_BUCKETS - 1)
    return jnp.where(n < max_exact, n, large)


def _bias_tiles(tbl):
    T = ATT_TILE
    assert (SAT_TILE - 1) * T + 1 >= MAX_DISTANCE
    nh = tbl.shape[1]
    o = jnp.arange(SAT_TILE + 1)[:, None, None]
    i = jnp.arange(T)[None, :, None]
    j = jnp.arange(T)[None, None, :]
    dist = o * T + i - j
    onehot = (_t5_bucket(dist)[None] == jnp.arange(NUM_BUCKETS)[:, None, None, None]).astype(F32)
    tiles = jnp.dot(tbl.astype(F32).T, onehot.reshape(NUM_BUCKETS, -1), precision=HIGHEST)
    tiles = jnp.where((dist >= 0)[None], tiles.reshape(nh, SAT_TILE + 1, T, T) * LOG2E, NEG_BIG)
    return jnp.concatenate([tiles, jnp.full((nh, 1, T, T), NEG_BIG, F32)], axis=1)


def _half_masks(shape):
    lane = lax.broadcasted_iota(jnp.int32, shape, len(shape) - 1)
    lo = lane < HEAD_DIM
    return lo, jnp.logical_not(lo)


def _rms_halves(x, gain2):
    lo, hi = _half_masks(x.shape)
    sq = x * x
    s_lo = jnp.sum(jnp.where(lo, sq, 0.0), -1, keepdims=True)
    s_hi = jnp.sum(jnp.where(hi, sq, 0.0), -1, keepdims=True)
    inv = jnp.where(lo, lax.rsqrt(s_lo / HEAD_DIM + RMS_EPS), lax.rsqrt(s_hi / HEAD_DIM + RMS_EPS))
    return x * inv * gain2


def _lane_chunks(x):
    return [x[:, c * LANES:(c + 1) * LANES] for c in range(x.shape[1] // LANES)]


def _flash(q2, kn, v_ref, bias_tile, qi, s_bufs, m_s, l_s, acc_s):
    T = ATT_TILE
    U = ATT_UNROLL
    H = Q_HALVES
    n_iter = (qi * H + H - 1) // U + 1

    def window(i):
        return pl.ds(pl.multiple_of(jnp.minimum(i, n_iter - 1) * (U * T), U * T), U * T)

    def produce(i, dst):
        s = lax.dot_general(q2, kn[window(i), :], (((1,), (1,)), ((), ())), preferred_element_type=F32)
        idx = {}
        for d in range(-(U - 1), H):
            off = qi * H - i * U + d
            idx[d] = jnp.where(off < 0, MASK_TILE, jnp.minimum(off, SAT_TILE))
        bias = jnp.concatenate(
            [jnp.concatenate([bias_tile(g, idx[a - u]) for u in range(U)], axis=1)
             for g in range(2) for a in range(H)], axis=0)
        dst[...] = s + bias

    def consume(i, src):
        parts = _lane_chunks(src[...])
        m_old = m_s[...]
        blk_max = jnp.max(functools.reduce(jnp.maximum, parts), -1, keepdims=True)
        m_new = jnp.maximum(m_old, jnp.broadcast_to(blk_max, m_old.shape))
        alpha = jnp.exp2(m_old - m_new)
        ps = [jnp.exp2(part - m_new) for part in parts]
        l_s[...] = alpha * l_s[...] + functools.reduce(lambda a, b: a + b, ps)
        p = jnp.concatenate(ps, axis=1).astype(BF16)
        acc_s[...] = alpha * acc_s[...] + jnp.dot(p, v_ref[0, window(i), :], preferred_element_type=F32)
        m_s[...] = m_new

    m_s[...] = jnp.full_like(m_s, -jnp.inf)
    l_s[...] = jnp.zeros_like(l_s)
    acc_s[...] = jnp.zeros_like(acc_s)
    produce(0, s_bufs[0])

    def body(i, carry):
        for par in range(2):
            @pl.when(i % 2 == par)
            def _():
                produce(i + 1, s_bufs[1 - par])
                consume(i, s_bufs[par])
        return carry

    lax.fori_loop(0, n_iter, body, 0)
    return acc_s[...] / jnp.sum(l_s[...], -1, keepdims=True)


def _diff_attn_kernel(q_ref, k_ref, v_ref, bias_ref, lam_ref, sg_ref, o_ref,
                      s_a, s_b, m_s, l_s, acc_s, *, lambda_init):
    qi = pl.program_id(2)
    q = q_ref[0]
    tq = q.shape[0]
    lo, hi = _half_masks(q.shape)
    zero = jnp.zeros_like(q)
    q2 = jnp.concatenate([jnp.where(lo, q, zero), jnp.where(hi, q, zero)], axis=0)

    def bias_tile(g, idx):
        return bias_ref[0, idx]

    o = _flash(q2, k_ref.at[0], v_ref, bias_tile, qi, (s_a, s_b), m_s, l_s, acc_s)
    lam = lam_ref[...]
    lam_full = (jnp.exp(jnp.sum(lam[0:1] * lam[1:2], -1, keepdims=True))
                - jnp.exp(jnp.sum(lam[2:3] * lam[3:4], -1, keepdims=True)) + lambda_init)
    out = o[:tq] - lam_full * o[tq:]
    out = out * lax.rsqrt(jnp.mean(out * out, -1, keepdims=True) + RMS_EPS) * sg_ref[...]
    o_ref[0] = (out * (1.0 - lambda_init)).astype(BF16)


def _diff_attn(p_d, bias_tiles, lam, subln_g, lambda_init):
    B, S, _ = p_d.shape
    T = ATT_TILE
    H = DF_HEADS
    TQ = Q_HALVES * T
    R = 2 * TQ
    assert S % (ATT_UNROLL * T) == 0 and S % TQ == 0
    kern = functools.partial(_diff_attn_kernel, lambda_init=lambda_init)
    return pl.pallas_call(
        kern,
        grid=(B, H, S // TQ),
        in_specs=[pl.BlockSpec((1, TQ, LANES), lambda b, h, i: (b, i, h)),
                  pl.BlockSpec((1, S, LANES), lambda b, h, i: (b, 0, H + h)),
                  pl.BlockSpec((1, S, LANES), lambda b, h, i: (b, 0, 2 * H + h)),
                  pl.BlockSpec((1, N_BIAS_TILES, T, T), lambda b, h, i: (h, 0, 0, 0)),
                  pl.BlockSpec((4, HEAD_DIM), lambda b, h, i: (0, 0)),
                  pl.BlockSpec((1, DF_V), lambda b, h, i: (0, 0))],
        out_specs=pl.BlockSpec((1, TQ, DF_V), lambda b, h, i: (b, i, h)),
        out_shape=jax.ShapeDtypeStruct((B, S, DF_WIDTH), BF16),
        scratch_shapes=[pltpu.VMEM((R, ATT_UNROLL * T), F32),
                        pltpu.VMEM((R, ATT_UNROLL * T), F32),
                        pltpu.VMEM((R, LANES), F32),
                        pltpu.VMEM((R, LANES), F32),
                        pltpu.VMEM((R, DF_V), F32)],
        compiler_params=_cparams(("parallel", "parallel", "arbitrary")),
        name="diff_attn",
    )(p_d, p_d, p_d, bias_tiles, lam, subln_g.reshape(1, DF_V))


def _moba_kernel(q_ref, k_ref, v_ref, bias_ref, o_ref, kn, kmean, s_a, s_b, m_s, l_s, acc_s):
    qi = pl.program_id(2)
    T = ATT_TILE
    nkb = k_ref.shape[1] // T

    @pl.when(qi == 0)
    def _():
        kmean[...] = jnp.zeros_like(kmean)
        lane = lax.broadcasted_iota(jnp.int32, (T, LANES), 1)

        def body(j, carry):
            rows = pl.ds(pl.multiple_of(j * T, T), T)
            kb = k_ref[0, rows, :]
            kn[rows, :] = jnp.concatenate([kb, (lane == j).astype(BF16)], axis=1)
            kmean[pl.ds(j, 1), :] = jnp.mean(kb.astype(F32), 0, keepdims=True)
            return carry
        lax.fori_loop(0, nkb, body, 0)

    q = q_ref[0]
    tq = q.shape[0]
    lo, hi = _half_masks(q.shape)
    zero = jnp.zeros_like(q)
    q2b = jnp.concatenate([jnp.where(lo, q, zero), jnp.where(hi, q, zero)], axis=0)

    nkp = kmean.shape[0]
    g2 = lax.dot_general(jnp.concatenate(_split_bf16(kmean[...], 2), axis=0), q2b, (((1,), (1,)), ((), ())),
                         preferred_element_type=F32)
    gate = g2[:nkp] + g2[nkp:]
    blk = lax.broadcasted_iota(jnp.int32, gate.shape, 0).astype(F32)
    col = lax.broadcasted_iota(jnp.int32, gate.shape, 1)
    own = (qi * Q_HALVES + (col % tq) // T).astype(F32)
    past = blk < own
    gate = jnp.where(past, gate, -jnp.inf)
    pen_t = jnp.where(past, NEG_BIG, 0.0)
    for _ in range(MB_TOPK):
        mx = jnp.max(gate, 0, keepdims=True)
        first = jnp.min(jnp.where(gate == mx, blk, float(nkp)), 0, keepdims=True)
        pick = jnp.logical_and(blk == first, mx > -jnp.inf)
        pen_t = jnp.where(pick, 0.0, pen_t)
        gate = jnp.where(pick, -jnp.inf, gate)
    pen = jnp.concatenate([pen_t, jnp.zeros((LANES - nkp, 2 * tq), F32)], axis=0).T
    q2 = jnp.concatenate([q2b, pen.astype(BF16)], axis=1)

    def bias_tile(g, idx):
        return bias_ref[g, idx]

    o = _flash(q2, kn, v_ref, bias_tile, qi, (s_a, s_b), m_s, l_s, acc_s)
    lo_o, _ = _half_masks((tq, LANES))
    o_ref[0] = jnp.where(lo_o, o[:tq], o[tq:]).astype(BF16)


def _moba(p_m, bias_tiles):
    B, S, _ = p_m.shape
    T = ATT_TILE
    HP = MB_HEADS // 2
    TQ = Q_HALVES * T
    R = 2 * TQ
    assert T == MB_BLOCK and S // T <= LANES and S % (ATT_UNROLL * T) == 0 and S % TQ == 0
    return pl.pallas_call(
        _moba_kernel,
        grid=(B, HP, S // TQ),
        in_specs=[pl.BlockSpec((1, TQ, LANES), lambda b, h, i: (b, i, h)),
                  pl.BlockSpec((1, S, LANES), lambda b, h, i: (b, 0, HP + h)),
                  pl.BlockSpec((1, S, LANES), lambda b, h, i: (b, 0, 2 * HP + h)),
                  pl.BlockSpec((2, N_BIAS_TILES, T, T), lambda b, h, i: (h, 0, 0, 0))],
        out_specs=pl.BlockSpec((1, TQ, LANES), lambda b, h, i: (b, i, h)),
        out_shape=jax.ShapeDtypeStruct((B, S, MB_WIDTH), BF16),
        scratch_shapes=[pltpu.VMEM((S, 2 * LANES), BF16),
                        pltpu.VMEM((-(-(S // T) // 8) * 8, LANES), F32),
                        pltpu.VMEM((R, ATT_UNROLL * T), F32),
                        pltpu.VMEM((R, ATT_UNROLL * T), F32),
                        pltpu.VMEM((R, LANES), F32),
                        pltpu.VMEM((R, LANES), F32),
                        pltpu.VMEM((R, LANES), F32)],
        compiler_params=_cparams(("parallel", "parallel", "arbitrary")),
        name="moba",
    )(p_m, p_m, p_m, bias_tiles)


def _pack_pairs(x):
    n = x.shape[1] // 2
    hi = pltpu.bitcast(x[:, :n].astype(BF16).astype(F32), jnp.uint32)
    lo = pltpu.bitcast(x[:, n:].astype(BF16).astype(F32), jnp.uint32)
    return hi | (lo >> 16)


def _unpack_pairs(p):
    hi = pltpu.bitcast(p & jnp.uint32(0xFFFF0000), F32)
    lo = pltpu.bitcast(p << 16, F32)
    return jnp.concatenate([hi, lo], axis=1)


def _outproj_kernel(yr_ref, yd_ref, ym_ref, x_ref, g1_ref, sc_ref, sh_ref, ng_ref, w_ref, wr_ref, br_ref,
                    x1_ref, h2_ref, route_ref, cnt_ref, carry):
    c1 = R_WIDTH
    c2 = R_WIDTH + DF_WIDTH
    mix = (jnp.dot(yr_ref[0], w_ref[:c1, :], preferred_element_type=F32)
           + jnp.dot(yd_ref[0], w_ref[c1:c2, :], preferred_element_type=F32)
           + jnp.dot(ym_ref[0], w_ref[c2:, :], preferred_element_type=F32))
    x1 = x_ref[0] + g1_ref[0] * mix
    x1_ref[0] = x1
    y = x1 * lax.rsqrt(jnp.mean(x1 * x1, -1, keepdims=True) + RMS_EPS)
    h2 = (y * ng_ref[...]) * (1.0 + sc_ref[0]) + sh_ref[0]
    h2_ref[0] = _pack_pairs(h2)

    logits = _mm_3pass(h2, wr_ref[...]) + br_ref[...]
    lane = lax.broadcasted_iota(jnp.int32, logits.shape, 1).astype(F32)
    far = float(LANES)
    is_g = lane < N_GROUPS
    gl = jnp.where(is_g, logits, -jnp.inf)
    gmax = jnp.max(gl, -1, keepdims=True)
    g_idx = jnp.min(jnp.where(gl == gmax, lane, far), -1, keepdims=True)
    pg_top = 1.0 / jnp.sum(jnp.where(is_g, jnp.exp(logits - gmax), 0.0), -1, keepdims=True)
    e_lo = N_GROUPS + EXPERTS_PER_GROUP * g_idx
    in_grp = jnp.logical_and(lane >= e_lo, lane < e_lo + EXPERTS_PER_GROUP)
    el = jnp.where(in_grp, logits, -jnp.inf)
    m1 = jnp.max(el, -1, keepdims=True)
    i1 = jnp.min(jnp.where(el == m1, lane, far), -1, keepdims=True)
    el2 = jnp.where(lane == i1, -jnp.inf, el)
    m2 = jnp.max(el2, -1, keepdims=True)
    i2 = jnp.min(jnp.where(el2 == m2, lane, far), -1, keepdims=True)
    e2 = jnp.exp(m2 - m1)
    gate1 = pg_top / (1.0 + e2)
    gate2 = gate1 * e2
    route = jnp.where(lane == 0, gate1, 0.0)
    route = jnp.where(lane == 1, gate2, route)
    route = jnp.where(lane == 2, i1 - N_GROUPS, route)
    route = jnp.where(lane == 3, i2 - N_GROUPS, route)

    first_step = jnp.logical_and(pl.program_id(0) == 0, pl.program_id(1) == 0)

    @pl.when(first_step)
    def _():
        carry[...] = jnp.zeros_like(carry)

    tm = logits.shape[0]
    oh0 = (lane == i1 - N_GROUPS).astype(F32)
    oh1 = (lane == i2 - N_GROUPS).astype(F32)
    oh = oh0 + oh1
    row = lax.broadcasted_iota(jnp.int32, (tm, tm), 0)
    col = lax.broadcasted_iota(jnp.int32, (tm, tm), 1)
    before = jnp.dot((col < row).astype(BF16), oh.astype(BF16), preferred_element_type=F32) + carry[...]
    route = jnp.where(lane == 4, jnp.sum(oh0 * before, -1, keepdims=True), route)
    route = jnp.where(lane == 5, jnp.sum(oh1 * before, -1, keepdims=True), route)
    route_ref[0] = route
    carry[...] = carry[...] + jnp.sum(oh, 0, keepdims=True)
    cnt_ref[...] = carry[...]


def _outproj(y_r, y_d, y_m, x, g1, sc2, sh2, norm2_g, w_out_bf16, w_router, b_router, tm=512):
    B, S, D = x.shape
    vec = pl.BlockSpec((1, 1, D), lambda b, i: (b, 0, 0))
    tok = lambda w: pl.BlockSpec((1, tm, w), lambda b, i: (b, i, 0))
    full = lambda a: pl.BlockSpec(a.shape, lambda b, i: (0,) * a.ndim)
    return pl.pallas_call(
        _outproj_kernel,
        grid=(B, S // tm),
        in_specs=[tok(R_WIDTH), tok(DF_WIDTH), tok(MB_WIDTH), tok(D), vec, vec, vec,
                  pl.BlockSpec((1, D), lambda b, i: (0, 0)), full(w_out_bf16), full(w_router), full(b_router)],
        out_specs=[tok(D), tok(D // 2), tok(LANES), pl.BlockSpec((1, LANES), lambda b, i: (0, 0))],
        out_shape=[jax.ShapeDtypeStruct((B, S, D), F32),
                   jax.ShapeDtypeStruct((B, S, D // 2), jnp.uint32),
                   jax.ShapeDtypeStruct((B, S, LANES), F32),
                   jax.ShapeDtypeStruct((1, LANES), F32)],
        scratch_shapes=[pltpu.VMEM((1, LANES), F32)],
        compiler_params=_cparams(("arbitrary", "arbitrary")),
        name="outproj_router",
    )(y_r, y_d, y_m, x, g1.reshape(B, 1, D), sc2.reshape(B, 1, D), sh2.reshape(B, 1, D),
      norm2_g.reshape(1, D), w_out_bf16, w_router, b_router)


def _dispatch_kernel(dest_ref, h_ref, buf_in, buf_out, sem):
    del buf_in
    tm = h_ref.shape[0]

    def row_copy(t, slot):
        return pltpu.make_async_copy(h_ref.at[pl.ds(t, 1)], buf_out.at[pl.ds(slot, 1)], sem)

    def issue(g, carry):
        base = pl.multiple_of(g * SUBLANES, SUBLANES)
        for r in range(SUBLANES):
            for k in range(TOP_K_INNER):
                row_copy(base + r, dest_ref[TOP_K_INNER * (base + r) + k]).start(priority=k % 2)
        return carry

    def drain(g, carry):
        for _ in range(SUBLANES * TOP_K_INNER):
            row_copy(0, 0).wait()
        return carry

    lax.fori_loop(0, tm // SUBLANES, issue, 0)
    lax.fori_loop(0, tm // SUBLANES, drain, 0)


def _dispatch(h2p, dest_flat, n_slots, tm=512):
    T, W = h2p.shape
    buf0 = jnp.zeros((n_slots, W), jnp.uint32)
    return pl.pallas_call(
        _dispatch_kernel,
        grid=(T // tm,),
        in_specs=[pl.BlockSpec((TOP_K_INNER * tm,), lambda i: (i,), memory_space=pltpu.SMEM),
                  pl.BlockSpec((tm, W), lambda i: (i, 0)),
                  pl.BlockSpec(memory_space=pl.ANY)],
        out_specs=pl.BlockSpec(memory_space=pl.ANY),
        out_shape=jax.ShapeDtypeStruct((n_slots, W), jnp.uint32),
        scratch_shapes=[pltpu.SemaphoreType.DMA],
        input_output_aliases={2: 0},
        compiler_params=_cparams(("arbitrary",)),
        name="moe_dispatch",
    )(dest_flat, h2p, buf0)


def _expert_kernel(be_ref, slot_ref, nxt_ref, nu_ref, x_ref, w1_hbm, w3_hbm, w2_hbm, o_ref,
                   w1_s, w3_s, w2_s, sems, *, layer):
    i = pl.program_id(0)
    used = i < nu_ref[0]
    e = be_ref[i]
    slot = slot_ref[i]
    run_start = jnp.logical_or(i == 0, e != be_ref[jnp.maximum(i - 1, 0)])

    def fetch(expert, s):
        return [pltpu.make_async_copy(w1_hbm.at[layer, expert], w1_s.at[s], sems.at[s, 0]),
                pltpu.make_async_copy(w3_hbm.at[layer, expert], w3_s.at[s], sems.at[s, 1]),
                pltpu.make_async_copy(w2_hbm.at[layer, expert], w2_s.at[s], sems.at[s, 2])]

    @pl.when(jnp.logical_and(used, run_start))
    def _():
        @pl.when(i == 0)
        def _():
            for cp in fetch(e, slot):
                cp.start()
        for cp in fetch(e, slot):
            cp.wait()
        nxt = nxt_ref[i]

        @pl.when(nxt >= 0)
        def _():
            for cp in fetch(nxt, 1 - slot):
                cp.start()

    @pl.when(used)
    def _():
        x = _unpack_pairs(x_ref[...]).astype(BF16)
        h1 = jnp.dot(x, w1_s[slot].astype(BF16), preferred_element_type=F32)
        h3 = jnp.dot(x, w3_s[slot].astype(BF16), preferred_element_type=F32)
        h = (h1 * jax.nn.sigmoid(h1) * h3).astype(BF16)
        o_ref[...] = _pack_pairs(jnp.dot(h, w2_s[slot].astype(BF16), preferred_element_type=F32))

    @pl.when(jnp.logical_not(used))
    def _():
        o_ref[...] = jnp.zeros_like(o_ref)


def _experts(buf, blk_e, counts, n_used, w1, w3, w2, layer):
    n_slots, W = buf.shape
    _, E, D, Hd = w1.shape
    nb = n_slots // MOE_BLOCK
    change = jnp.concatenate([jnp.ones((1,), jnp.int32), (blk_e[1:] != blk_e[:-1]).astype(jnp.int32)])
    slot = (jnp.cumsum(change) - 1) % 2
    ids = jnp.arange(E, dtype=jnp.int32)
    later = jnp.logical_and(ids[None, :] > ids[:, None], counts[None, :] > 0)
    nxt_e = jnp.min(jnp.where(later, ids[None, :], E), axis=1)
    nxt_e = jnp.where(nxt_e == E, -1, nxt_e)
    nxt = jnp.sum(jnp.where(blk_e[:, None] == ids[None, :], nxt_e[None, :], 0), axis=1)
    any_spec = pl.BlockSpec(memory_space=pl.ANY)
    grid_spec = pltpu.PrefetchScalarGridSpec(
        num_scalar_prefetch=4,
        grid=(nb,),
        in_specs=[pl.BlockSpec((MOE_BLOCK, W), lambda i, *_: (i, 0)), any_spec, any_spec, any_spec],
        out_specs=pl.BlockSpec((MOE_BLOCK, W), lambda i, *_: (i, 0)),
        scratch_shapes=[pltpu.VMEM((2, D, Hd), F32), pltpu.VMEM((2, D, Hd), F32),
                        pltpu.VMEM((2, Hd, D), F32), pltpu.SemaphoreType.DMA((2, 3))],
    )
    return pl.pallas_call(
        functools.partial(_expert_kernel, layer=layer),
        grid_spec=grid_spec,
        out_shape=jax.ShapeDtypeStruct((n_slots, W), jnp.uint32),
        compiler_params=_cparams(("arbitrary",)),
        name="moe_experts",
    )(blk_e, slot.astype(jnp.int32), nxt.astype(jnp.int32), n_used, buf, w1, w3, w2)


def _combine_kernel(dest_ref, dest_nxt_ref, yb_ref, route_ref, x_ref, g2_ref, *rest, next_ln):
    if next_ln:
        ng_ref, sc_ref, sh_ref, w_ref, gains_ref, o_ref, pr_ref, pd_ref, pm_ref, rows, sems = rest
    else:
        o_ref, rows, sems = rest
    i = pl.program_id(0)
    n = pl.num_programs(0)
    tm = x_ref.shape[0]
    groups = tm // SUBLANES
    cur = i % 2

    def row_copy(buf, t, k, src_row):
        return pltpu.make_async_copy(yb_ref.at[pl.ds(src_row, 1)], rows.at[buf, k, pl.ds(t, 1)], sems.at[buf])

    def start_tile(d_ref, buf):
        def group(g, carry):
            base = pl.multiple_of(g * SUBLANES, SUBLANES)
            for r in range(SUBLANES):
                for k in range(TOP_K_INNER):
                    row_copy(buf, base + r, k, d_ref[TOP_K_INNER * (base + r) + k]).start(priority=k % 2)
            return carry
        lax.fori_loop(0, groups, group, 0)

    def wait_tile(buf):
        def group(g, carry):
            for _ in range(SUBLANES * TOP_K_INNER):
                row_copy(buf, 0, 0, 0).wait()
            return carry
        lax.fori_loop(0, groups, group, 0)

    @pl.when(i == 0)
    def _():
        start_tile(dest_ref, 0)

    @pl.when(i + 1 < n)
    def _():
        start_tile(dest_nxt_ref, 1 - cur)

    wait_tile(cur)
    rt = route_ref[...]
    y = rt[:, 0:1] * _unpack_pairs(rows[cur, 0]) + rt[:, 1:2] * _unpack_pairs(rows[cur, 1])
    x2 = x_ref[...] + g2_ref[0] * y
    o_ref[...] = x2
    if next_ln:
        yn = x2 * lax.rsqrt(jnp.mean(x2 * x2, -1, keepdims=True) + RMS_EPS)
        h = ((yn * ng_ref[...]) * (1.0 + sc_ref[0]) + sh_ref[0]).astype(BF16)
        pr_ref[...], pd_ref[...], pm_ref[...] = _project(h, w_ref, gains_ref)


def _combine(yb, dest_flat, route, x1, g2, seq, next_ln=None, tm=512):
    T, D = x1.shape
    W = yb.shape[1]
    per = seq // tm
    B = T // seq
    n = T // tm
    tok = lambda w: pl.BlockSpec((tm, w), lambda i: (i, 0))
    vec = pl.BlockSpec((1, 1, D), lambda i: (i // per, 0, 0))
    smem_cur = pl.BlockSpec((TOP_K_INNER * tm,), lambda i: (i,), memory_space=pltpu.SMEM)
    smem_nxt = pl.BlockSpec((TOP_K_INNER * tm,), lambda i: (jnp.minimum(i + 1, n - 1),),
                            memory_space=pltpu.SMEM)
    in_specs = [smem_cur, smem_nxt, pl.BlockSpec(memory_space=pl.ANY), tok(LANES), tok(D), vec]
    args = [dest_flat, dest_flat, yb, route, x1, g2.reshape(B, 1, D)]
    out_specs = [tok(D)]
    out_shape = [jax.ShapeDtypeStruct((T, D), F32)]
    if next_ln is not None:
        ng, sc, sh, w_bf16, gains = next_ln
        in_specs += [pl.BlockSpec((1, D), lambda i: (0, 0)), vec, vec,
                     pl.BlockSpec((D, IN_COLS), lambda i: (0, 0)),
                     pl.BlockSpec(gains.shape, lambda i: (0, 0))]
        args += [ng.reshape(1, D), sc.reshape(B, 1, D), sh.reshape(B, 1, D), w_bf16, gains]
        out_specs += [tok(R_COLS), tok(DF_COLS), tok(MB_COLS)]
        out_shape += [jax.ShapeDtypeStruct((T, R_COLS), F32), jax.ShapeDtypeStruct((T, DF_COLS), BF16),
                      jax.ShapeDtypeStruct((T, MB_COLS), BF16)]
    return pl.pallas_call(
        functools.partial(_combine_kernel, next_ln=next_ln is not None),
        grid=(n,),
        in_specs=in_specs,
        out_specs=out_specs,
        out_shape=out_shape,
        scratch_shapes=[pltpu.VMEM((2, TOP_K_INNER, tm, W), jnp.uint32), pltpu.SemaphoreType.DMA((2,))],
        compiler_params=_cparams(("arbitrary",)),
        name="moe_combine_ln" if next_ln is not None else "moe_combine",
    )(*args)


def _moe(h2p, route, cnt, x1, g2, w1, w3, w2, seq, layer, next_ln=None):
    T, D = x1.shape
    A = T * TOP_K_INNER
    n_blocks = -(-(A + N_EXPERTS * (MOE_BLOCK - 1)) // MOE_BLOCK)
    rank = route[:, 4:4 + TOP_K_INNER]
    counts = cnt[0, :N_EXPERTS].astype(jnp.int32)
    padded = (counts + MOE_BLOCK - 1) // MOE_BLOCK * MOE_BLOCK
    pad_end = jnp.cumsum(padded)
    pad_start = pad_end - padded
    experts = route[:, 2:2 + TOP_K_INNER].astype(jnp.int32)
    is_e = experts[:, :, None] == jnp.arange(N_EXPERTS, dtype=jnp.int32)
    dest = (jnp.sum(jnp.where(is_e, pad_start, 0), axis=-1) + rank[:, :TOP_K_INNER].astype(jnp.int32)).reshape(A)
    blk_first = jnp.arange(n_blocks, dtype=jnp.int32) * MOE_BLOCK
    blk_e = jnp.minimum(jnp.sum((pad_end[None, :] <= blk_first[:, None]).astype(jnp.int32), axis=1),
                        N_EXPERTS - 1)
    n_used = (pad_end[-1:] // MOE_BLOCK).astype(jnp.int32)
    buf = _dispatch(h2p, dest, n_blocks * MOE_BLOCK)
    yb = _experts(buf, blk_e, counts, n_used, w1, w3, w2, layer)
    return _combine(yb, dest, route, x1, g2, seq, next_ln)


def kernel(x, c, ada_w, ada_b, norm1_g, norm2_g, w_in, w_out, rwkv_mu, rwkv_w0, rwkv_w2, rwkv_a0, rwkv_a2, rwkv_g2, rwkv_kk, rwkv_ka, rwkv_rk, rwkv_ln_g, rwkv_ln_b, diff_q_gain, diff_k_gain, diff_lambda, diff_subln_g, moba_q_gain, moba_k_gain, rel_bias, router_g_w, router_g_b, router_e_w, router_e_b, moe_w1, moe_w3, moe_w2):
    B, S, D = x.shape
    T = B * S
    mods = _adaln(c, ada_w, ada_b)
    bias_df = _bias_tiles(rel_bias[:, :DF_HEADS])
    bias_mb = _bias_tiles(rel_bias[:, DF_HEADS:])
    n_route = N_GROUPS + N_EXPERTS
    mod = lambda l, j: mods[l, :, j * D:(j + 1) * D]
    def qk_gains(l):
        two = lambda g, s: jnp.concatenate([g, g]) * s
        return jnp.stack([two(diff_q_gain[l], QK_SCALE), two(diff_k_gain[l], 1.0),
                          two(moba_q_gain[l], QK_SCALE), two(moba_k_gain[l], 1.0)])

    p_r, p_d, p_m = _ln_inproj(x, norm1_g[0], mod(0, 1), mod(0, 0), w_in[0].astype(BF16), qk_gains(0))
    for l in range(DEPTH):
        g1, sh2, sc2, g2 = mod(l, 2), mod(l, 3), mod(l, 4), mod(l, 5)
        r, k, v, kap, b, lw, g, bonus = _rwkv_prep(p_r, rwkv_mu[l], rwkv_w0[l], rwkv_w2[l], rwkv_a0[l],
                                                   rwkv_a2[l], rwkv_g2[l], rwkv_kk[l], rwkv_ka[l],
                                                   rwkv_rk[l])
        y_r = _rwkv_chunk(r, k, v, kap, b, lw, g, bonus, rwkv_ln_g[l], rwkv_ln_b[l])
        lambda_init = 0.8 - 0.6 * math.exp(-0.3 * l)
        y_d = _diff_attn(p_d, bias_df, diff_lambda[l], diff_subln_g[l], lambda_init)
        y_m = _moba(p_m, bias_mb)
        w_router = jnp.zeros((D, LANES), F32).at[:, :n_route].set(
            jnp.concatenate([router_g_w[l], router_e_w[l]], axis=1))
        b_router = jnp.zeros((1, LANES), F32).at[0, :n_route].set(
            jnp.concatenate([router_g_b[l], router_e_b[l]]))
        x1, h2p, route, cnt = _outproj(y_r, y_d, y_m, x, g1, sc2, sh2, norm2_g[l], w_out[l].astype(BF16),
                                       w_router, b_router)
        moe_args = (h2p.reshape(T, D // 2), route.reshape(T, LANES), cnt, x1.reshape(T, D), g2,
                    moe_w1, moe_w3, moe_w2, S, l)
        if l + 1 < DEPTH:
            nxt = (norm1_g[l + 1], mod(l + 1, 1), mod(l + 1, 0), w_in[l + 1].astype(BF16), qk_gains(l + 1))
            x2, p_r, p_d, p_m = _moe(*moe_args, next_ln=nxt)
            p_r, p_d, p_m = (p_r.reshape(B, S, R_COLS), p_d.reshape(B, S, DF_COLS),
                             p_m.reshape(B, S, MB_COLS))
        else:
            (x2,) = _moe(*moe_args)
        x = x2.reshape(B, S, D)
    return x
```
